```python
import math
import jax, jax.numpy as jnp
from jax import lax
import numpy as np

D_MODEL = 1024
BATCH = 4
SEQ = 8192
DEPTH = 2

CHUNK = 64
N_META = 16
SB_HEADS = 8
SB_HEAD_DIM = 64
SB_WIDTH = SB_HEADS * SB_HEAD_DIM
CONV_CH = D_MODEL // 2
CONV_K = 31
D_FF = 2816
Q_BLOCK = 128
NORM_EPS = 1e-6
LN_EPS = 1e-5
IN_SIZES = [SB_WIDTH, SB_WIDTH, SB_WIDTH, CONV_CH, CONV_CH, D_MODEL, D_MODEL]
IN_SPLITS = [int(s) for s in np.cumsum(IN_SIZES)[:-1]]
IN_WIDTH = int(sum(IN_SIZES))

kernel_name = "hybrid_stickbreak_conformer_macaron"


def rmsnorm(x, g):
    xf = x.astype(jnp.float32)
    y = xf * lax.rsqrt(jnp.mean(xf * xf, axis=-1, keepdims=True) + NORM_EPS)
    return (y * g.astype(jnp.float32)).astype(x.dtype)


def layernorm(x, g, b):
    xf = x.astype(jnp.float32)
    mu = jnp.mean(xf, axis=-1, keepdims=True)
    var = jnp.mean(jnp.square(xf - mu), axis=-1, keepdims=True)
    y = (xf - mu) * lax.rsqrt(var + LN_EPS)
    return (y * g.astype(jnp.float32) + b.astype(jnp.float32)).astype(x.dtype)


def swiglu(x, w_gate, w_up, w_down):
    return (jax.nn.silu(x @ w_gate) * (x @ w_up)) @ w_down


def stick_breaking_attention(q, k, v):
    B, L, _ = q.shape
    Lp = ((L + Q_BLOCK - 1) // Q_BLOCK) * Q_BLOCK
    nb = Lp // Q_BLOCK

    def heads(a):
        a = a.reshape(B, L, SB_HEADS, SB_HEAD_DIM).transpose(0, 2, 1, 3).astype(jnp.float32)
        return jnp.pad(a, ((0, 0), (0, 0), (0, Lp - L), (0, 0)))

    qh, kh, vh = heads(q), heads(k), heads(v)
    scale = 1.0 / math.sqrt(SB_HEAD_DIM)
    q_blocks = qh.reshape(B, SB_HEADS, nb, Q_BLOCK, SB_HEAD_DIM).transpose(2, 0, 1, 3, 4)
    starts = jnp.arange(nb, dtype=jnp.int32) * Q_BLOCK
    key_pos = jnp.arange(Lp, dtype=jnp.int32)[None, :]

    def block(args):
        qb, start = args
        z = jnp.einsum('bhqd,bhkd->bhqk', qb, kh) * scale
        qpos = start + jnp.arange(Q_BLOCK, dtype=jnp.int32)[:, None]
        valid = key_pos < qpos
        log_keep = jnp.where(valid, jax.nn.log_sigmoid(-z), 0.0)
        after = lax.cumsum(log_keep, axis=3, reverse=True) - log_keep
        attn = jnp.where(valid, jnp.exp(jax.nn.log_sigmoid(z) + after), 0.0)
        return jnp.einsum('bhqk,bhkd->bhqd', attn, vh)

    out = lax.map(block, (q_blocks, starts))
    out = out.transpose(1, 2, 0, 3, 4).reshape(B, SB_HEADS, Lp, SB_HEAD_DIM)[:, :, :L]
    return out.transpose(0, 2, 1, 3).reshape(B, L, SB_WIDTH).astype(q.dtype)


def conformer_conv(a, b, conv_w, conv_b, ln_g, ln_b):
    u = a * jax.nn.sigmoid(b)
    y = lax.conv_general_dilated(
        u, conv_w[:, None, :].astype(u.dtype), window_strides=(1,),
        padding=[(CONV_K - 1, 0)],
        dimension_numbers=('NWC', 'WIO', 'NWC'), feature_group_count=CONV_CH)
    y = y + conv_b.astype(y.dtype)
    return jax.nn.silu(layernorm(y, ln_g, ln_b))


def hybrid_mixer(u, w_in, w_attn_out, conv_w, conv_b, conv_ln_g, conv_ln_b, w_conv_out, w_out):
    proj = u @ w_in
    q, k, v, a, b, g_sb, g_conv = jnp.split(proj, IN_SPLITS, axis=-1)
    y_sb = stick_breaking_attention(q, k, v) @ w_attn_out
    y_conv = conformer_conv(a, b, conv_w, conv_b, conv_ln_g, conv_ln_b) @ w_conv_out
    m = jax.nn.sigmoid(g_sb) * y_sb + jax.nn.sigmoid(g_conv) * y_conv
    return m @ w_out


def setup_inputs(seed: int = 0) -> dict:
    key = jax.random.key(seed)
    ks = iter(jax.random.split(key, 32))

    def w(shape, fan_in, scale=1.0):
        return jax.random.normal(next(ks), shape, jnp.float32) * (scale * fan_in ** -0.5)

    def gain(shape):
        return 1.0 + 0.05 * jax.random.normal(next(ks), shape, jnp.float32)

    def bias(shape):
        return 0.02 * jax.random.normal(next(ks), shape, jnp.float32)

    L_ = DEPTH
    return {
        "x": jax.random.normal(next(ks), (BATCH, SEQ, D_MODEL), jnp.float32),
        "meta": jax.random.normal(next(ks), (N_META, D_MODEL), jnp.float32),
        "ffn1_norm": gain((L_, D_MODEL)),
        "ffn1_w_gate": w((L_, D_MODEL, D_FF), D_MODEL),
        "ffn1_w_up": w((L_, D_MODEL, D_FF), D_MODEL),
        "ffn1_w_down": w((L_, D_FF, D_MODEL), D_FF),
        "mix_norm": gain((L_, D_MODEL)),
        "w_in": w((L_, D_MODEL, IN_WIDTH), D_MODEL),
        "w_attn_out": w((L_, SB_WIDTH, D_MODEL), SB_WIDTH),
        "conv_w": w((L_, CONV_K, CONV_CH), CONV_K),
        "conv_b": bias((L_, CONV_CH)),
        "conv_ln_g": gain((L_, CONV_CH)),
        "conv_ln_b": bias((L_, CONV_CH)),
        "w_conv_out": w((L_, CONV_CH, D_MODEL), CONV_CH),
        "w_out": w((L_, D_MODEL, D_MODEL), D_MODEL),
        "ffn2_norm": gain((L_, D_MODEL)),
        "ffn2_w_gate": w((L_, D_MODEL, D_FF), D_MODEL),
        "ffn2_w_up": w((L_, D_MODEL, D_FF), D_MODEL),
        "ffn2_w_down": w((L_, D_FF, D_MODEL), D_FF),
        "final_norm": gain((D_MODEL,)),
    }


def reference(x, meta, ffn1_norm, ffn1_w_gate, ffn1_w_up, ffn1_w_down, mix_norm, w_in,
              w_attn_out, conv_w, conv_b, conv_ln_g, conv_ln_b, w_conv_out, w_out,
              ffn2_norm, ffn2_w_gate, ffn2_w_up, ffn2_w_down, final_norm):
    B = x.shape[0]
    meta_b = jnp.broadcast_to(meta.astype(x.dtype)[None], (B, N_META, D_MODEL))
    h = jnp.concatenate([meta_b, x], axis=1)
    for l in range(DEPTH):
        h = h + 0.5 * swiglu(rmsnorm(h, ffn1_norm[l]), ffn1_w_gate[l], ffn1_w_up[l], ffn1_w_down[l])
        h = h + hybrid_mixer(rmsnorm(h, mix_norm[l]), w_in[l], w_attn_out[l], conv_w[l], conv_b[l],
                             conv_ln_g[l], conv_ln_b[l], w_conv_out[l], w_out[l])
        h = h + 0.5 * swiglu(rmsnorm(h, ffn2_norm[l]), ffn2_w_gate[l], ffn2_w_up[l], ffn2_w_down[l])
    h = rmsnorm(h, final_norm)
    return h[:, N_META:]
```

```python
import functools
import math

import jax
import jax.numpy as jnp
from jax import lax
from jax.experimental import pallas as pl
from jax.experimental.pallas import tpu as pltpu

D_MODEL = 1024
N_META = 16
SB_HEADS = 8
SB_HEAD_DIM = 64
SB_WIDTH = SB_HEADS * SB_HEAD_DIM
CONV_CH = 512
CONV_K = 31
D_FF = 2816
NORM_EPS = 1e-6
LN_EPS = 1e-5

ATT_BLOCK = 128
FF_CHUNK = 256
CONV_HALO = 32
VMEM_LIMIT = 56 * 1024 * 1024

SKIP_LOG = -104.0

F32 = jnp.float32
BF16 = jnp.bfloat16


def _rmsnorm(x, g):
    return x * lax.rsqrt(jnp.mean(x * x, axis=-1, keepdims=True) + NORM_EPS) * g


def _sigmoid(x):
    return 1.0 / (1.0 + jnp.exp(-x))


def _resident(shape):
    return pl.BlockSpec(shape, lambda *_: (0,) * len(shape), pipeline_mode=pl.Buffered(1))


def _params(n_axes):
    return pltpu.CompilerParams(dimension_semantics=("arbitrary",) * n_axes,
                                vmem_limit_bytes=VMEM_LIMIT)


def _ffn_body(h_ref, g_ref, wg_ref, wu_ref, wd_ref, *rest, final):
    if final:
        fg_ref, o_ref, act_ref = rest
    else:
        o_ref, act_ref = rest
    x = h_ref[...]
    xn = _rmsnorm(x, g_ref[...]).astype(BF16)
    for c0 in range(0, D_FF, FF_CHUNK):
        gate = jnp.dot(xn, wg_ref[:, c0:c0 + FF_CHUNK], preferred_element_type=F32)
        up = jnp.dot(xn, wu_ref[:, c0:c0 + FF_CHUNK], preferred_element_type=F32)
        act_ref[:, c0:c0 + FF_CHUNK] = (gate * _sigmoid(gate) * up).astype(BF16)
    y = x + 0.5 * jnp.dot(act_ref[...], wd_ref[...], preferred_element_type=F32)
    if final:
        y = _rmsnorm(y, fg_ref[...])
    o_ref[...] = y


def _ffn(h, g, wg, wu, wd, final_g=None, *, tm):
    rows = h.shape[0]
    final = final_g is not None
    row_spec = pl.BlockSpec((tm, D_MODEL), lambda i: (i, 0))
    in_specs = [row_spec, _resident((1, D_MODEL)), _resident((D_MODEL, D_FF)),
                _resident((D_MODEL, D_FF)), _resident((D_FF, D_MODEL))]
    args = [h, g, wg, wu, wd]
    if final:
        in_specs.append(_resident((1, D_MODEL)))
        args.append(final_g)
    return pl.pallas_call(
        functools.partial(_ffn_body, final=final),
        grid=(rows // tm,),
        in_specs=in_specs,
        out_specs=row_spec,
        out_shape=jax.ShapeDtypeStruct((rows, D_MODEL), F32),
        scratch_shapes=[pltpu.VMEM((tm, D_FF), BF16)],
        compiler_params=_params(1),
        name="ffn_final" if final else "ffn",
    )(*args)


def _proj_body(h_ref, g_ref, w_ref, q_ref, k_ref, v_ref, u_ref, *, tm, pad):
    xn = _rmsnorm(h_ref[...], g_ref[...]).astype(BF16)

    def col(i):
        return jnp.dot(xn, w_ref[:, i * SB_WIDTH:(i + 1) * SB_WIDTH], preferred_element_type=F32)

    q_ref[...] = (col(0) * (1.0 / math.sqrt(SB_HEAD_DIM))).astype(BF16)
    k_ref[...] = col(1).astype(BF16)
    v_ref[...] = col(2).astype(BF16)
    u = col(3) * _sigmoid(col(4))
    pos = pl.program_id(1) * tm + lax.broadcasted_iota(jnp.int32, (tm, 1), 0)
    u_ref[...] = jnp.where(pos >= pad, u, 0.0)


def _proj(h, g, w, *, batch, lp, pad, tm):
    rows = h.shape[0]
    nt = lp // tm
    row = lambda width: pl.BlockSpec((tm, width), lambda b, j: (b * nt + j, 0))
    bf = jax.ShapeDtypeStruct((rows, SB_WIDTH), BF16)
    return pl.pallas_call(
        functools.partial(_proj_body, tm=tm, pad=pad),
        grid=(batch, nt),
        in_specs=[row(D_MODEL), _resident((1, D_MODEL)), _resident(w.shape)],
        out_specs=[row(SB_WIDTH), row(SB_WIDTH), row(SB_WIDTH), row(CONV_CH)],
        out_shape=[bf, bf, bf, jax.ShapeDtypeStruct((rows, CONV_CH), F32)],
        compiler_params=_params(2),
        name="mixer_proj",
    )(h, g, w)


def _attn_body(q_ref, k_ref, v_ref, o_ref, *, pad):
    blk = ATT_BLOCK
    i = pl.program_id(1)
    row = lax.broadcasted_iota(jnp.int32, (blk, blk), 0)
    lane = lax.broadcasted_iota(jnp.int32, (blk, blk), 1)
    qpos = i * blk + row
    later = (row > lane).astype(BF16)

    for h in range(SB_HEADS):
        cols = slice(h * SB_HEAD_DIM, (h + 1) * SB_HEAD_DIM)
        qh = q_ref[:, cols]

        def body(state, cols=cols, qh=qh):
            j, carry, acc, _ = state
            start = pl.multiple_of(j * blk, blk)
            kh = k_ref[pl.ds(start, blk), cols]
            vh = v_ref[pl.ds(start, blk), cols]
            z = lax.dot_general(qh, kh, (((1,), (1,)), ((), ())), preferred_element_type=F32)
            kpos = start + lane
            valid = (kpos < qpos) & (kpos >= pad)
            softplus = jnp.maximum(z, 0.0) + jnp.log1p(jnp.exp(-jnp.abs(z)))
            log_keep = jnp.where(valid, -softplus, 0.0)
            hi = log_keep.astype(BF16)
            lo = (log_keep - hi.astype(F32)).astype(BF16)
            after = (jnp.dot(hi, later, preferred_element_type=F32)
                     + jnp.dot(lo, later, preferred_element_type=F32) + carry)
            p = jnp.exp(jnp.where(valid, z + log_keep + after, -1e30))
            acc = acc + jnp.dot(p.astype(BF16), vh, preferred_element_type=F32)
            carry = carry + jnp.sum(log_keep, axis=1, keepdims=True)
            return j - 1, carry, acc, (jnp.max(carry) > SKIP_LOG).astype(jnp.int32)

        def cond(state):
            j, _, _, live = state
            return (j >= 0) & (live > 0)

        init = (i, jnp.zeros((blk, 1), F32), jnp.zeros((blk, SB_HEAD_DIM), F32), jnp.int32(1))
        _, _, acc, _ = lax.while_loop(cond, body, init)
        o_ref[:, cols] = acc.astype(BF16)


def _attention(q, k, v, *, batch, lp, pad):
    nq = lp // ATT_BLOCK
    qspec = pl.BlockSpec((ATT_BLOCK, SB_WIDTH), lambda b, i: (b * nq + i, 0))
    kvspec = pl.BlockSpec((lp, SB_WIDTH), lambda b, i: (b, 0), pipeline_mode=pl.Buffered(1))
    return pl.pallas_call(
        functools.partial(_attn_body, pad=pad),
        grid=(batch, nq),
        in_specs=[qspec, kvspec, kvspec],
        out_specs=qspec,
        out_shape=jax.ShapeDtypeStruct(q.shape, BF16),
        compiler_params=_params(2),
        name="sb_attention",
    )(q, k, v)


def _mix_body(h_ref, g_ref, wgate_ref, a_ref, u_ref, uprev_ref, cw_ref, cb_ref, lng_ref, lnb_ref,
              wao_ref, wco_ref, wo_ref, o_ref, ubuf_ref, *, tm):
    ubuf_ref[0:CONV_HALO, :] = uprev_ref[...]
    ubuf_ref[CONV_HALO:, :] = u_ref[...]
    base = CONV_HALO - (CONV_K - 1)
    y = cb_ref[...] + cw_ref[0:1, :] * ubuf_ref[base:base + tm, :]
    for t in range(1, CONV_K):
        y = y + cw_ref[t:t + 1, :] * ubuf_ref[base + t:base + t + tm, :]
    mu = jnp.mean(y, axis=-1, keepdims=True)
    yc = y - mu
    var = jnp.mean(yc * yc, axis=-1, keepdims=True)
    ln = yc * lax.rsqrt(var + LN_EPS) * lng_ref[...] + lnb_ref[...]
    conv = (ln * _sigmoid(ln)).astype(BF16)

    x = h_ref[...]
    xn = _rmsnorm(x, g_ref[...]).astype(BF16)
    gate_sb = _sigmoid(jnp.dot(xn, wgate_ref[:, :D_MODEL], preferred_element_type=F32))
    gate_cv = _sigmoid(jnp.dot(xn, wgate_ref[:, D_MODEL:], preferred_element_type=F32))
    y_sb = jnp.dot(a_ref[...], wao_ref[...], preferred_element_type=F32)
    y_cv = jnp.dot(conv, wco_ref[...], preferred_element_type=F32)
    m = (gate_sb * y_sb + gate_cv * y_cv).astype(BF16)
    o_ref[...] = x + jnp.dot(m, wo_ref[...], preferred_element_type=F32)


def _mix(h, g, wgate, a, u, cw, cb, lng, lnb, wao, wco, wo, *, tm):
    rows = h.shape[0]
    halo_per_tile = tm // CONV_HALO
    row = lambda width: pl.BlockSpec((tm, width), lambda i: (i, 0))
    prev = pl.BlockSpec((CONV_HALO, CONV_CH), lambda i: (jnp.maximum(i * halo_per_tile - 1, 0), 0))
    return pl.pallas_call(
        functools.partial(_mix_body, tm=tm),
        grid=(rows // tm,),
        in_specs=[row(D_MODEL), _resident((1, D_MODEL)), _resident(wgate.shape), row(SB_WIDTH),
                  row(CONV_CH), prev, _resident(cw.shape), _resident((1, CONV_CH)),
                  _resident((1, CONV_CH)), _resident((1, CONV_CH)), _resident(wao.shape),
                  _resident(wco.shape), _resident(wo.shape)],
        out_specs=row(D_MODEL),
        out_shape=jax.ShapeDtypeStruct((rows, D_MODEL), F32),
        scratch_shapes=[pltpu.VMEM((CONV_HALO + tm, CONV_CH), F32)],
        compiler_params=_params(1),
        name="mixer_out",
    )(h, g, wgate, a, u, u, cw, cb, lng, lnb, wao, wco, wo)


def kernel(x, meta, ffn1_norm, ffn1_w_gate, ffn1_w_up, ffn1_w_down, mix_norm, w_in, w_attn_out,
           conv_w, conv_b, conv_ln_g, conv_ln_b, w_conv_out, w_out, ffn2_norm, ffn2_w_gate,
           ffn2_w_up, ffn2_w_down, final_norm):
    batch, seq, d = x.shape
    depth = w_in.shape[0]
    assert d == D_MODEL and meta.shape == (N_META, D_MODEL) and seq % ATT_BLOCK == 0
    assert N_META <= ATT_BLOCK
    pad = ATT_BLOCK - N_META
    lp = seq + ATT_BLOCK
    rows = batch * lp
    tm_ffn = 512
    tm_mix = 640
    assert rows % tm_ffn == 0 and lp % tm_mix == 0 and tm_mix % CONV_HALO == 0

    head = jnp.concatenate([jnp.zeros((pad, d), x.dtype), meta.astype(x.dtype)], axis=0)
    h = jnp.concatenate([jnp.broadcast_to(head[None], (batch, ATT_BLOCK, d)), x], axis=1)
    h = h.reshape(rows, d)

    bf = lambda w: w.astype(BF16)
    vec = lambda p: p.reshape(1, -1).astype(F32)
    n_proj = 3 * SB_WIDTH + 2 * CONV_CH
    for l in range(depth):
        h = _ffn(h, vec(ffn1_norm[l]), bf(ffn1_w_gate[l]), bf(ffn1_w_up[l]), bf(ffn1_w_down[l]),
                 tm=tm_ffn)
        q, k, v, u = _proj(h, vec(mix_norm[l]), bf(w_in[l, :, :n_proj]),
                           batch=batch, lp=lp, pad=pad, tm=tm_mix)
        a = _attention(q, k, v, batch=batch, lp=lp, pad=pad)
        h = _mix(h, vec(mix_norm[l]), bf(w_in[l, :, n_proj:]), a, u, conv_w[l].astype(F32),
                 vec(conv_b[l]), vec(conv_ln_g[l]), vec(conv_ln_b[l]), bf(w_attn_out[l]),
                 bf(w_conv_out[l]), bf(w_out[l]), tm=tm_mix)
        h = _ffn(h, vec(ffn2_norm[l]), bf(ffn2_w_gate[l]), bf(ffn2_w_up[l]), bf(ffn2_w_down[l]),
                 vec(final_norm) if l == depth - 1 else None, tm=tm_ffn)
    return h.reshape(batch, lp, d)[:, ATT_BLOCK:]
```

```python
import functools
import math

import jax
import jax.numpy as jnp
from jax import lax
from jax.experimental import pallas as pl
from jax.experimental.pallas import tpu as pltpu

D_MODEL = 1024
N_META = 16
SB_HEADS = 8
SB_HEAD_DIM = 64
SB_WIDTH = SB_HEADS * SB_HEAD_DIM
HEAD_PAIR = 2 * SB_HEAD_DIM
CONV_CH = 512
CONV_K = 31
D_FF = 2816
NORM_EPS = 1e-6
LN_EPS = 1e-5
LOG2E = 1.4426950408889634

F32_SUBLANES = 8
ATT_BLOCK = 128
ATT_FIXED = 3
FF_CHUNK = 256
CONV_HALO = 32
CONV_CHUNK = 32
VMEM_LIMIT = 56 * 1024 * 1024

SKIP_LOG = -104.0

F32 = jnp.float32
BF16 = jnp.bfloat16


def _rmsnorm(x, g):
    return x * lax.rsqrt(jnp.mean(x * x, axis=-1, keepdims=True) + NORM_EPS) * g


def _sigmoid(x):
    return 1.0 / (1.0 + jnp.exp(-x))


def _resident(shape):
    return pl.BlockSpec(shape, lambda *_: (0,) * len(shape), pipeline_mode=pl.Buffered(1))


def _params(n_axes):
    return pltpu.CompilerParams(dimension_semantics=("arbitrary",) * n_axes,
                                vmem_limit_bytes=VMEM_LIMIT)


def _ffn_body(h_ref, g_ref, wg_ref, wu_ref, wd_ref, *rest, final):
    if final:
        fg_ref, o_ref, act_ref = rest
    else:
        o_ref, act_ref = rest
    x = h_ref[...]
    xn = _rmsnorm(x, g_ref[...]).astype(BF16)
    for c0 in range(0, D_FF, FF_CHUNK):
        gate = jnp.dot(xn, wg_ref[:, c0:c0 + FF_CHUNK], preferred_element_type=F32)
        up = jnp.dot(xn, wu_ref[:, c0:c0 + FF_CHUNK], preferred_element_type=F32)
        act_ref[:, c0:c0 + FF_CHUNK] = (gate * _sigmoid(gate) * up).astype(BF16)
    y = x + 0.5 * jnp.dot(act_ref[...], wd_ref[...], preferred_element_type=F32)
    if final:
        y = _rmsnorm(y, fg_ref[...])
    o_ref[...] = y


def _ffn(h, g, wg, wu, wd, final_g=None, *, tm):
    rows = h.shape[0]
    final = final_g is not None
    row_spec = pl.BlockSpec((tm, D_MODEL), lambda i: (i, 0))
    in_specs = [row_spec, _resident((1, D_MODEL)), _resident((D_MODEL, D_FF)),
                _resident((D_MODEL, D_FF)), _resident((D_FF, D_MODEL))]
    args = [h, g, wg, wu, wd]
    if final:
        in_specs.append(_resident((1, D_MODEL)))
        args.append(final_g)
    return pl.pallas_call(
        functools.partial(_ffn_body, final=final),
        grid=(rows // tm,),
        in_specs=in_specs,
        out_specs=row_spec,
        out_shape=jax.ShapeDtypeStruct((rows, D_MODEL), F32),
        scratch_shapes=[pltpu.VMEM((tm, D_FF), BF16)],
        compiler_params=_params(1),
        name="ffn_final" if final else "ffn",
    )(*args)


def _proj_body(h_ref, g_ref, w_ref, q_ref, k_ref, v_ref, u_ref, *, tm, pad):
    xn = _rmsnorm(h_ref[...], g_ref[...]).astype(BF16)

    def col(i):
        return jnp.dot(xn, w_ref[:, i * SB_WIDTH:(i + 1) * SB_WIDTH], preferred_element_type=F32)

    q_ref[...] = (col(0) * (-1.0 / math.sqrt(SB_HEAD_DIM))).astype(BF16)
    k_ref[...] = col(1).astype(BF16)
    v_ref[...] = col(2).astype(BF16)
    u = col(3) * _sigmoid(col(4))
    pos = pl.program_id(1) * tm + lax.broadcasted_iota(jnp.int32, (tm, 1), 0)
    u_ref[...] = jnp.where(pos >= pad, u, 0.0)


def _proj(h, g, w, *, batch, lp, pad, tm):
    rows = h.shape[0]
    nt = lp // tm
    row = lambda width: pl.BlockSpec((tm, width), lambda b, j: (b * nt + j, 0))
    bf = jax.ShapeDtypeStruct((rows, SB_WIDTH), BF16)
    return pl.pallas_call(
        functools.partial(_proj_body, tm=tm, pad=pad),
        grid=(batch, nt),
        in_specs=[row(D_MODEL), _resident((1, D_MODEL)), _resident(w.shape)],
        out_specs=[row(SB_WIDTH), row(SB_WIDTH), row(SB_WIDTH), row(CONV_CH)],
        out_shape=[bf, bf, bf, jax.ShapeDtypeStruct((rows, CONV_CH), F32)],
        compiler_params=_params(2),
        name="mixer_proj",
    )(h, g, w)


def _attn_body(q_ref, k_ref, v_ref, o_ref, acc_ref, carry_ref, *, pad):
    blk = ATT_BLOCK
    i = pl.program_id(1)
    row = lax.broadcasted_iota(jnp.int32, (blk, blk), 0)
    lane = lax.broadcasted_iota(jnp.int32, (blk, blk), 1)
    qpos = i * blk + row
    half = jnp.concatenate([(row > lane).astype(BF16), jnp.ones((blk, blk), BF16)], axis=1)
    suffix_and_total = jnp.concatenate([half, half], axis=0)
    first = (lane < SB_HEAD_DIM).astype(BF16)
    second = (lane >= SB_HEAD_DIM).astype(BF16)
    nt = (((1,), (1,)), ((), ()))

    pair_cols = [slice(m * HEAD_PAIR, (m + 1) * HEAD_PAIR) for m in range(SB_HEADS // 2)]

    def sweep(blocks, carry, acc):
        valid, k_diag, v_diag = [], [], []
        for j in blocks:
            start = pl.multiple_of(jnp.maximum(j, 0) * blk, blk)
            kpos = start + lane
            valid.append((kpos < qpos) & (kpos >= pad) & (j >= 0))
            kt = k_ref[pl.ds(start, blk), :]
            vt = v_ref[pl.ds(start, blk), :]
            k_diag.append([jnp.concatenate([kt[:, c] * first, kt[:, c] * second], axis=0)
                           for c in pair_cols])
            v_diag.append([jnp.concatenate([vt[:, c] * first, vt[:, c] * second], axis=0)
                           for c in pair_cols])
        zneg = [[lax.dot_general(q_ref[:, c], kd, nt, preferred_element_type=F32)
                 for c, kd in zip(pair_cols, kds)] for kds in k_diag]
        log_keep, split = [], []
        for b in range(len(blocks)):
            for h in range(SB_HEADS):
                zn = zneg[b][h // 2][:, (h % 2) * blk:(h % 2 + 1) * blk]
                lk = jnp.minimum(zn, 0.0) - jnp.log(1.0 + jnp.exp2(jnp.abs(zn) * (-LOG2E)))
                lk = jnp.where(valid[b], lk, 0.0)
                hi = lk.astype(BF16)
                lo = (lk - hi.astype(F32)).astype(BF16)
                log_keep.append(lk)
                split.append(jnp.concatenate([hi, lo], axis=1))
        sums = jnp.dot(jnp.concatenate(split, axis=0), suffix_and_total,
                       preferred_element_type=F32)
        carry, acc = list(carry), list(acc)
        for b in range(len(blocks)):
            probs = []
            for h in range(SB_HEADS):
                r0 = (b * SB_HEADS + h) * blk
                after = sums[r0:r0 + blk, :blk]
                total = sums[r0:r0 + blk, blk:]
                if carry[h] is not None:
                    after = after + carry[h]
                zn = zneg[b][h // 2][:, (h % 2) * blk:(h % 2 + 1) * blk]
                log_w = log_keep[b * SB_HEADS + h] - zn + after
                probs.append(jnp.exp(jnp.where(valid[b], log_w, -1e30)).astype(BF16))
                carry[h] = total if carry[h] is None else carry[h] + total
            for m in range(SB_HEADS // 2):
                out = jnp.dot(jnp.concatenate(probs[2 * m:2 * m + 2], axis=1), v_diag[b][m],
                              preferred_element_type=F32)
                acc[m] = out if acc[m] is None else acc[m] + out
        return carry, acc

    def store_state(carry, acc):
        for h in range(SB_HEADS):
            carry_ref[h] = carry[h]
        for m, c in enumerate(pair_cols):
            acc_ref[:, c] = acc[m]
        return (jnp.max(functools.reduce(jnp.maximum, carry)) > SKIP_LOG).astype(jnp.int32)

    live = store_state(*sweep([i - d for d in range(ATT_FIXED)],
                              [None] * SB_HEADS, [None] * (SB_HEADS // 2)))

    def body(state):
        j, _ = state
        carry, acc = sweep([j], [carry_ref[h] for h in range(SB_HEADS)],
                           [acc_ref[:, c] for c in pair_cols])
        return j - 1, store_state(carry, acc)

    lax.while_loop(lambda s: (s[0] >= 0) & (s[1] > 0), body, (i - ATT_FIXED, live))
    o_ref[...] = acc_ref[...].astype(BF16)


def _attention(q, k, v, *, batch, lp, pad):
    nq = lp // ATT_BLOCK
    qspec = pl.BlockSpec((ATT_BLOCK, SB_WIDTH), lambda b, i: (b * nq + i, 0))
    kvspec = pl.BlockSpec((lp, SB_WIDTH), lambda b, i: (b, 0), pipeline_mode=pl.Buffered(1))
    return pl.pallas_call(
        functools.partial(_attn_body, pad=pad),
        grid=(batch, nq),
        in_specs=[qspec, kvspec, kvspec],
        out_specs=qspec,
        out_shape=jax.ShapeDtypeStruct(q.shape, BF16),
        scratch_shapes=[pltpu.VMEM((ATT_BLOCK, SB_WIDTH), F32),
                        pltpu.VMEM((SB_HEADS, ATT_BLOCK, ATT_BLOCK), F32)],
        compiler_params=_params(2),
        name="sb_attention",
    )(q, k, v)


def _mix_body(h_ref, g_ref, wgate_ref, a_ref, u_ref, uprev_ref, cw_ref, cb_ref, lng_ref, lnb_ref,
              wao_ref, wco_ref, wo_ref, o_ref, ubuf_ref, shift_ref, conv_ref, *, tm):
    ubuf_ref[0:CONV_HALO, :] = uprev_ref[...]
    ubuf_ref[CONV_HALO:, :] = u_ref[...]
    base = CONV_HALO - (CONV_K - 1)
    sub = F32_SUBLANES
    for r in range(sub):
        n = tm + sub * ((CONV_K - 1 - r) // sub)
        shift_ref[r, 0:n // sub] = ubuf_ref[base + r:base + r + n, :].reshape(n // sub, sub, CONV_CH)
    for c0 in range(0, tm, CONV_CHUNK):
        y = jnp.broadcast_to(cb_ref[...], (CONV_CHUNK // sub, sub, CONV_CH))
        for k in range(CONV_K):
            g0 = (c0 + k - k % sub) // sub
            y = y + cw_ref[k] * shift_ref[k % sub, g0:g0 + CONV_CHUNK // sub]
        y = y.reshape(CONV_CHUNK, CONV_CH)
        mu = jnp.mean(y, axis=-1, keepdims=True)
        yc = y - mu
        var = jnp.mean(yc * yc, axis=-1, keepdims=True)
        ln = yc * lax.rsqrt(var + LN_EPS) * lng_ref[...] + lnb_ref[...]
        conv_ref[c0:c0 + CONV_CHUNK, :] = (ln * _sigmoid(ln)).astype(BF16)

    x = h_ref[...]
    xn = _rmsnorm(x, g_ref[...]).astype(BF16)
    gate_sb = _sigmoid(jnp.dot(xn, wgate_ref[:, :D_MODEL], preferred_element_type=F32))
    gate_cv = _sigmoid(jnp.dot(xn, wgate_ref[:, D_MODEL:], preferred_element_type=F32))
    y_sb = jnp.dot(a_ref[...], wao_ref[...], preferred_element_type=F32)
    y_cv = jnp.dot(conv_ref[...], wco_ref[...], preferred_element_type=F32)
    m = (gate_sb * y_sb + gate_cv * y_cv).astype(BF16)
    o_ref[...] = x + jnp.dot(m, wo_ref[...], preferred_element_type=F32)


def _mix(h, g, wgate, a, u, cw, cb, lng, lnb, wao, wco, wo, *, tm):
    rows = h.shape[0]
    halo_per_tile = tm // CONV_HALO
    row = lambda width: pl.BlockSpec((tm, width), lambda i: (i, 0))
    prev = pl.BlockSpec((CONV_HALO, CONV_CH), lambda i: (jnp.maximum(i * halo_per_tile - 1, 0), 0))
    shift_rows = tm + F32_SUBLANES * ((CONV_K - 1) // F32_SUBLANES)
    return pl.pallas_call(
        functools.partial(_mix_body, tm=tm),
        grid=(rows // tm,),
        in_specs=[row(D_MODEL), _resident((1, D_MODEL)), _resident(wgate.shape), row(SB_WIDTH),
                  row(CONV_CH), prev, _resident(cw.shape), _resident((1, CONV_CH)),
                  _resident((1, CONV_CH)), _resident((1, CONV_CH)), _resident(wao.shape),
                  _resident(wco.shape), _resident(wo.shape)],
        out_specs=row(D_MODEL),
        out_shape=jax.ShapeDtypeStruct((rows, D_MODEL), F32),
        scratch_shapes=[pltpu.VMEM((CONV_HALO + tm, CONV_CH), F32),
                        pltpu.VMEM((F32_SUBLANES, shift_rows // F32_SUBLANES, F32_SUBLANES, CONV_CH), F32),
                        pltpu.VMEM((tm, CONV_CH), BF16)],
        compiler_params=_params(1),
        name="mixer_out",
    )(h, g, wgate, a, u, u, cw, cb, lng, lnb, wao, wco, wo)


def kernel(x, meta, ffn1_norm, ffn1_w_gate, ffn1_w_up, ffn1_w_down, mix_norm, w_in, w_attn_out,
           conv_w, conv_b, conv_ln_g, conv_ln_b, w_conv_out, w_out, ffn2_norm, ffn2_w_gate,
           ffn2_w_up, ffn2_w_down, final_norm):
    batch, seq, d = x.shape
    depth = w_in.shape[0]
    assert d == D_MODEL and meta.shape == (N_META, D_MODEL) and seq % ATT_BLOCK == 0
    assert N_META <= ATT_BLOCK
    pad = ATT_BLOCK - N_META
    lp = seq + ATT_BLOCK
    rows = batch * lp
    tm_ffn = 512
    tm_mix = 640
    assert rows % tm_ffn == 0 and lp % tm_mix == 0 and tm_mix % CONV_HALO == 0
    assert tm_mix % CONV_CHUNK == 0

    head = jnp.concatenate([jnp.zeros((pad, d), x.dtype), meta.astype(x.dtype)], axis=0)
    h = jnp.concatenate([jnp.broadcast_to(head[None], (batch, ATT_BLOCK, d)), x], axis=1)
    h = h.reshape(rows, d)

    bf = lambda w: w.astype(BF16)
    vec = lambda p: p.reshape(1, -1).astype(F32)
    conv_taps = lambda w: jnp.broadcast_to(w.astype(F32)[:, None, :], (CONV_K, F32_SUBLANES, CONV_CH))
    n_proj = 3 * SB_WIDTH + 2 * CONV_CH
    for l in range(depth):
        h = _ffn(h, vec(ffn1_norm[l]), bf(ffn1_w_gate[l]), bf(ffn1_w_up[l]), bf(ffn1_w_down[l]),
                 tm=tm_ffn)
        q, k, v, u = _proj(h, vec(mix_norm[l]), bf(w_in[l, :, :n_proj]),
                           batch=batch, lp=lp, pad=pad, tm=tm_mix)
        a = _attention(q, k, v, batch=batch, lp=lp, pad=pad)
        h = _mix(h, vec(mix_norm[l]), bf(w_in[l, :, n_proj:]), a, u, conv_taps(conv_w[l]),
                 vec(conv_b[l]), vec(conv_ln_g[l]), vec(conv_ln_b[l]), bf(w_attn_out[l]),
                 bf(w_conv_out[l]), bf(w_out[l]), tm=tm_mix)
        h = _ffn(h, vec(ffn2_norm[l]), bf(ffn2_w_gate[l]), bf(ffn2_w_up[l]), bf(ffn2_w_down[l]),
                 vec(final_norm) if l == depth - 1 else None, tm=tm_ffn)
    return h.reshape(batch, lp, d)[:, ATT_BLOCK:]
```

```python
import functools
import math

import jax
import jax.numpy as jnp
from jax import lax
from jax.experimental import pallas as pl
from jax.experimental.pallas import tpu as pltpu

D_MODEL = 1024
N_META = 16
SB_HEADS = 8
SB_HEAD_DIM = 64
SB_WIDTH = SB_HEADS * SB_HEAD_DIM
HEAD_PAIR = 2 * SB_HEAD_DIM
CONV_CH = 512
CONV_K = 31
D_FF = 2816
NORM_EPS = 1e-6
LN_EPS = 1e-5
LOG2E = 1.4426950408889634

F32_SUBLANES = 8
ATT_BLOCK = 128
ATT_FIXED = 3
MXU_COLS = 256
FF_CHUNK = MXU_COLS
CONV_HALO = 32
CONV_CHUNK = 32
MIX_ROWS = 320
VMEM_LIMIT = 56 * 1024 * 1024

SKIP_LOG = -104.0
MASKED = 1e30

F32 = jnp.float32
BF16 = jnp.bfloat16


def _rmsnorm(x, g):
    return x * lax.rsqrt(jnp.mean(x * x, axis=-1, keepdims=True) + NORM_EPS) * g


def _sigmoid(x):
    return 1.0 / (1.0 + jnp.exp(-x))


def _resident(shape):
    return pl.BlockSpec(shape, lambda *_: (0,) * len(shape), pipeline_mode=pl.Buffered(1))


def _params(n_axes, flags=None):
    return pltpu.CompilerParams(dimension_semantics=("arbitrary",) * n_axes,
                                vmem_limit_bytes=VMEM_LIMIT, flags=flags)


def _ffn_body(h_ref, g_ref, wg_ref, wu_ref, wd_ref, *rest, final):
    if final:
        fg_ref, o_ref, act_ref = rest
    else:
        o_ref, act_ref = rest
    x = h_ref[...]
    xn = _rmsnorm(x, g_ref[...]).astype(BF16)
    for c0 in range(0, D_FF, FF_CHUNK):
        c1 = min(c0 + FF_CHUNK, D_FF)
        gate = jnp.dot(xn, wg_ref[:, c0:c1], preferred_element_type=F32)
        up = jnp.dot(xn, wu_ref[:, c0:c1], preferred_element_type=F32)
        act_ref[:, c0:c1] = (gate * _sigmoid(gate) * up).astype(BF16)
    y = x + 0.5 * jnp.dot(act_ref[...], wd_ref[...], preferred_element_type=F32)
    if final:
        y = _rmsnorm(y, fg_ref[...])
    o_ref[...] = y


def _ffn(h, g, wg, wu, wd, final_g=None, *, tm):
    rows = h.shape[0]
    final = final_g is not None
    row_spec = pl.BlockSpec((tm, D_MODEL), lambda i: (i, 0))
    in_specs = [row_spec, _resident((1, D_MODEL)), _resident((D_MODEL, D_FF)),
                _resident((D_MODEL, D_FF)), _resident((D_FF, D_MODEL))]
    args = [h, g, wg, wu, wd]
    if final:
        in_specs.append(_resident((1, D_MODEL)))
        args.append(final_g)
    return pl.pallas_call(
        functools.partial(_ffn_body, final=final),
        grid=(rows // tm,),
        in_specs=in_specs,
        out_specs=row_spec,
        out_shape=jax.ShapeDtypeStruct((rows, D_MODEL), F32),
        scratch_shapes=[pltpu.VMEM((tm, D_FF), BF16)],
        compiler_params=_params(1),
        name="ffn_final" if final else "ffn",
    )(*args)


def _proj_body(h_ref, g_ref, w_ref, q_ref, k_ref, v_ref, u_ref, *, tm, pad):
    xn = _rmsnorm(h_ref[...], g_ref[...]).astype(BF16)

    def col(i):
        return jnp.dot(xn, w_ref[:, i * SB_WIDTH:(i + 1) * SB_WIDTH], preferred_element_type=F32)

    q_ref[...] = (col(0) * (-1.0 / math.sqrt(SB_HEAD_DIM))).astype(BF16)
    k_ref[...] = col(1).astype(BF16)
    v_ref[...] = col(2).astype(BF16)
    u = col(3) * _sigmoid(col(4))
    pos = pl.program_id(1) * tm + lax.broadcasted_iota(jnp.int32, (tm, 1), 0)
    u_ref[...] = jnp.where(pos >= pad, u, 0.0)


def _proj(h, g, w, *, batch, lp, pad, tm):
    rows = h.shape[0]
    nt = lp // tm
    row = lambda width: pl.BlockSpec((tm, width), lambda b, j: (b * nt + j, 0))
    bf = jax.ShapeDtypeStruct((rows, SB_WIDTH), BF16)
    return pl.pallas_call(
        functools.partial(_proj_body, tm=tm, pad=pad),
        grid=(batch, nt),
        in_specs=[row(D_MODEL), _resident((1, D_MODEL)), _resident(w.shape)],
        out_specs=[row(SB_WIDTH), row(SB_WIDTH), row(SB_WIDTH), row(CONV_CH)],
        out_shape=[bf, bf, bf, jax.ShapeDtypeStruct((rows, CONV_CH), F32)],
        compiler_params=_params(2),
        name="mixer_proj",
    )(h, g, w)


def _attn_body(q_ref, k_ref, v_ref, o_ref, acc_ref, carry_ref, *, pad):
    blk = ATT_BLOCK
    i = pl.program_id(1)
    row = lax.broadcasted_iota(jnp.int32, (blk, blk), 0)
    lane = lax.broadcasted_iota(jnp.int32, (blk, blk), 1)
    qpos = i * blk + row
    half = jnp.concatenate([(row > lane).astype(BF16), jnp.ones((blk, blk), BF16)], axis=1)
    suffix_and_total = jnp.concatenate([half, half], axis=0)
    first = (lane < SB_HEAD_DIM).astype(BF16)
    second = (lane >= SB_HEAD_DIM).astype(BF16)
    nt = (((1,), (1,)), ((), ()))

    pair_cols = [slice(m * HEAD_PAIR, (m + 1) * HEAD_PAIR) for m in range(SB_HEADS // 2)]

    def sweep(blocks, carry, acc):
        masked, k_diag, v_diag = [], [], []
        for j in blocks:
            start = pl.multiple_of(jnp.maximum(j, 0) * blk, blk)
            kpos = start + lane
            valid = (kpos < qpos) & (kpos >= pad) & (j >= 0)
            masked.append(jnp.where(valid, 0.0, MASKED))
            kt = k_ref[pl.ds(start, blk), :]
            vt = v_ref[pl.ds(start, blk), :]
            k_diag.append([jnp.concatenate([kt[:, c] * first, kt[:, c] * second], axis=0)
                           for c in pair_cols])
            v_diag.append([jnp.concatenate([vt[:, c] * first, vt[:, c] * second], axis=0)
                           for c in pair_cols])
        zneg = [[lax.dot_general(q_ref[:, c], kd, nt, preferred_element_type=F32)
                 for c, kd in zip(pair_cols, kds)] for kds in k_diag]
        log_beta, split = [], []
        for b in range(len(blocks)):
            for h in range(SB_HEADS):
                zn = zneg[b][h // 2][:, (h % 2) * blk:(h % 2 + 1) * blk] + masked[b]
                lk = jnp.minimum(zn, 0.0) - jnp.log(1.0 + jnp.exp2(jnp.abs(zn) * (-LOG2E)))
                hi = lk.astype(BF16)
                lo = (lk - hi.astype(F32)).astype(BF16)
                log_beta.append(lk - zn)
                split.append(jnp.concatenate([hi, lo], axis=1))
        sums = jnp.dot(jnp.concatenate(split, axis=0), suffix_and_total,
                       preferred_element_type=F32)
        carry, acc = list(carry), list(acc)
        for b in range(len(blocks)):
            probs = []
            for h in range(SB_HEADS):
                r0 = (b * SB_HEADS + h) * blk
                after = sums[r0:r0 + blk, :blk]
                total = sums[r0:r0 + blk, blk:]
                if carry[h] is not None:
                    after = after + carry[h]
                probs.append(jnp.exp(log_beta[b * SB_HEADS + h] + after).astype(BF16))
                carry[h] = total if carry[h] is None else carry[h] + total
            for m in range(SB_HEADS // 2):
                out = jnp.dot(jnp.concatenate(probs[2 * m:2 * m + 2], axis=1), v_diag[b][m],
                              preferred_element_type=F32)
                acc[m] = out if acc[m] is None else acc[m] + out
        return carry, acc

    def store_state(carry, acc):
        for h in range(SB_HEADS):
            carry_ref[h] = carry[h]
        for m, c in enumerate(pair_cols):
            acc_ref[:, c] = acc[m]
        return (jnp.max(functools.reduce(jnp.maximum, carry)) > SKIP_LOG).astype(jnp.int32)

    live = store_state(*sweep([i - d for d in range(ATT_FIXED)],
                              [None] * SB_HEADS, [None] * (SB_HEADS // 2)))

    def body(state):
        j, _ = state
        carry, acc = sweep([j], [carry_ref[h] for h in range(SB_HEADS)],
                           [acc_ref[:, c] for c in pair_cols])
        return j - 1, store_state(carry, acc)

    lax.while_loop(lambda s: (s[0] >= 0) & (s[1] > 0), body, (i - ATT_FIXED, live))
    o_ref[...] = acc_ref[...].astype(BF16)


def _attention(q, k, v, *, batch, lp, pad):
    nq = lp // ATT_BLOCK
    qspec = pl.BlockSpec((ATT_BLOCK, SB_WIDTH), lambda b, i: (b * nq + i, 0))
    kvspec = pl.BlockSpec((lp, SB_WIDTH), lambda b, i: (b, 0), pipeline_mode=pl.Buffered(1))
    return pl.pallas_call(
        functools.partial(_attn_body, pad=pad),
        grid=(batch, nq),
        in_specs=[qspec, kvspec, kvspec],
        out_specs=qspec,
        out_shape=jax.ShapeDtypeStruct(q.shape, BF16),
        scratch_shapes=[pltpu.VMEM((ATT_BLOCK, SB_WIDTH), F32),
                        pltpu.VMEM((SB_HEADS, ATT_BLOCK, ATT_BLOCK), F32)],
        compiler_params=_params(2),
        name="sb_attention",
    )(q, k, v)


def _mix_body(h_ref, g_ref, wgate_ref, a_ref, u_ref, uprev_ref, cw_ref, cb_ref, lng_ref, lnb_ref,
              wao_ref, wco_ref, wo_ref, o_ref, ubuf_ref, shift_ref, conv_ref, *, tm):
    ubuf_ref[0:CONV_HALO, :] = uprev_ref[...]
    ubuf_ref[CONV_HALO:, :] = u_ref[...]
    base = CONV_HALO - (CONV_K - 1)
    sub = F32_SUBLANES
    for r in range(sub):
        n = tm + sub * ((CONV_K - 1 - r) // sub)
        shift_ref[r, 0:n // sub] = ubuf_ref[base + r:base + r + n, :].reshape(n // sub, sub, CONV_CH)

    def conv_chunk(c0):
        y = jnp.broadcast_to(cb_ref[...], (CONV_CHUNK // sub, sub, CONV_CH))
        for k in range(CONV_K):
            g0 = (c0 + k - k % sub) // sub
            y = y + cw_ref[k] * shift_ref[k % sub, g0:g0 + CONV_CHUNK // sub]
        y = y.reshape(CONV_CHUNK, CONV_CH)
        mu = jnp.mean(y, axis=-1, keepdims=True)
        yc = y - mu
        var = jnp.mean(yc * yc, axis=-1, keepdims=True)
        ln = yc * lax.rsqrt(var + LN_EPS) * lng_ref[...] + lnb_ref[...]
        conv_ref[c0:c0 + CONV_CHUNK, :] = (ln * _sigmoid(ln)).astype(BF16)

    for r0 in range(0, tm, MIX_ROWS):
        rows = slice(r0, r0 + MIX_ROWS)
        for c0 in range(r0, r0 + MIX_ROWS, CONV_CHUNK):
            conv_chunk(c0)
        x = h_ref[rows, :]
        xn = _rmsnorm(x, g_ref[...]).astype(BF16)
        gate_sb = _sigmoid(jnp.dot(xn, wgate_ref[:, :D_MODEL], preferred_element_type=F32))
        gate_cv = _sigmoid(jnp.dot(xn, wgate_ref[:, D_MODEL:], preferred_element_type=F32))
        y_sb = jnp.dot(a_ref[rows, :], wao_ref[...], preferred_element_type=F32)
        y_cv = jnp.dot(conv_ref[rows, :], wco_ref[...], preferred_element_type=F32)
        m = (gate_sb * y_sb + gate_cv * y_cv).astype(BF16)
        o_ref[rows, :] = x + jnp.dot(m, wo_ref[...], preferred_element_type=F32)


def _mix(h, g, wgate, a, u, cw, cb, lng, lnb, wao, wco, wo, *, tm):
    rows = h.shape[0]
    halo_per_tile = tm // CONV_HALO
    row = lambda width: pl.BlockSpec((tm, width), lambda i: (i, 0))
    prev = pl.BlockSpec((CONV_HALO, CONV_CH), lambda i: (jnp.maximum(i * halo_per_tile - 1, 0), 0))
    shift_rows = tm + F32_SUBLANES * ((CONV_K - 1) // F32_SUBLANES)
    return pl.pallas_call(
        functools.partial(_mix_body, tm=tm),
        grid=(rows // tm,),
        in_specs=[row(D_MODEL), _resident((1, D_MODEL)), _resident(wgate.shape), row(SB_WIDTH),
                  row(CONV_CH), prev, _resident(cw.shape), _resident((1, CONV_CH)),
                  _resident((1, CONV_CH)), _resident((1, CONV_CH)), _resident(wao.shape),
                  _resident(wco.shape), _resident(wo.shape)],
        out_specs=row(D_MODEL),
        out_shape=jax.ShapeDtypeStruct((rows, D_MODEL), F32),
        scratch_shapes=[pltpu.VMEM((CONV_HALO + tm, CONV_CH), F32),
                        pltpu.VMEM((F32_SUBLANES, shift_rows // F32_SUBLANES, F32_SUBLANES, CONV_CH), F32),
                        pltpu.VMEM((tm, CONV_CH), BF16)],
        compiler_params=_params(1),
        name="mixer_out",
    )(h, g, wgate, a, u, u, cw, cb, lng, lnb, wao, wco, wo)


def kernel(x, meta, ffn1_norm, ffn1_w_gate, ffn1_w_up, ffn1_w_down, mix_norm, w_in, w_attn_out,
           conv_w, conv_b, conv_ln_g, conv_ln_b, w_conv_out, w_out, ffn2_norm, ffn2_w_gate,
           ffn2_w_up, ffn2_w_down, final_norm):
    batch, seq, d = x.shape
    depth = w_in.shape[0]
    assert d == D_MODEL and meta.shape == (N_META, D_MODEL) and seq % ATT_BLOCK == 0
    assert N_META <= ATT_BLOCK
    pad = ATT_BLOCK - N_META
    lp = seq + ATT_BLOCK
    rows = batch * lp
    tm_ffn = 1040
    tm_mix = 640
    assert rows % tm_ffn == 0 and lp % tm_mix == 0 and tm_mix % CONV_HALO == 0
    assert tm_mix % CONV_CHUNK == 0

    head = jnp.concatenate([jnp.zeros((pad, d), x.dtype), meta.astype(x.dtype)], axis=0)
    h = jnp.concatenate([jnp.broadcast_to(head[None], (batch, ATT_BLOCK, d)), x], axis=1)
    h = h.reshape(rows, d)

    bf = lambda w: w.astype(BF16)
    vec = lambda p: p.reshape(1, -1).astype(F32)
    conv_taps = lambda w: jnp.broadcast_to(w.astype(F32)[:, None, :], (CONV_K, F32_SUBLANES, CONV_CH))
    n_proj = 3 * SB_WIDTH + 2 * CONV_CH
    for l in range(depth):
        h = _ffn(h, vec(ffn1_norm[l]), bf(ffn1_w_gate[l]), bf(ffn1_w_up[l]), bf(ffn1_w_down[l]),
                 tm=tm_ffn)
        q, k, v, u = _proj(h, vec(mix_norm[l]), bf(w_in[l, :, :n_proj]),
                           batch=batch, lp=lp, pad=pad, tm=tm_mix)
        a = _attention(q, k, v, batch=batch, lp=lp, pad=pad)
        h = _mix(h, vec(mix_norm[l]), bf(w_in[l, :, n_proj:]), a, u, conv_taps(conv_w[l]),
                 vec(conv_b[l]), vec(conv_ln_g[l]), vec(conv_ln_b[l]), bf(w_attn_out[l]),
                 bf(w_conv_out[l]), bf(w_out[l]), tm=tm_mix)
        h = _ffn(h, vec(ffn2_norm[l]), bf(ffn2_w_gate[l]), bf(ffn2_w_up[l]), bf(ffn2_w_down[l]),
                 vec(final_norm) if l == depth - 1 else None, tm=tm_ffn)
    return h.reshape(batch, lp, d)[:, ATT_BLOCK:]
```

```python
import functools
import math

import jax
import jax.numpy as jnp
from jax import lax
from jax.experimental import pallas as pl
from jax.experimental.pallas import tpu as pltpu

D_MODEL = 1024
N_META = 16
SB_HEADS = 8
SB_HEAD_DIM = 64
SB_WIDTH = SB_HEADS * SB_HEAD_DIM
HEAD_PAIR = 2 * SB_HEAD_DIM
CONV_CH = 512
CONV_K = 31
D_FF = 2816
NORM_EPS = 1e-6
LN_EPS = 1e-5
LOG2E = 1.4426950408889634

F32_SUBLANES = 8
ATT_BLOCK = 128
ATT_FIXED = 3
MXU_COLS = 256
FF_CHUNK = MXU_COLS
CONV_HALO = 32
CONV_CHUNK = 32
MIX_ROWS = 320
VMEM_LIMIT = 56 * 1024 * 1024

SKIP_LOG = -104.0
MASKED = 1e30

F32 = jnp.float32
BF16 = jnp.bfloat16


def _rmsnorm(x, g):
    return x * lax.rsqrt(jnp.mean(x * x, axis=-1, keepdims=True) + NORM_EPS) * g


def _sigmoid(x):
    return 1.0 / (1.0 + jnp.exp(-x))


def _resident(shape):
    return pl.BlockSpec(shape, lambda *_: (0,) * len(shape), pipeline_mode=pl.Buffered(1))


def _params(n_axes):
    return pltpu.CompilerParams(dimension_semantics=("arbitrary",) * n_axes,
                                vmem_limit_bytes=VMEM_LIMIT)


def _ffn_body(h_ref, g_ref, wg_ref, wu_ref, wd_ref, *rest, final, tiles_per_batch):
    if final:
        fg_ref, o_ref, act_ref = rest
    elif tiles_per_batch:
        head_ref, o_ref, act_ref = rest
    else:
        o_ref, act_ref = rest
    x = h_ref[...]
    if tiles_per_batch and not final:
        n_head = head_ref.shape[0]
        first = jnp.concatenate([head_ref[...], x[:x.shape[0] - n_head]], axis=0)
        x = jnp.where(pl.program_id(0) % tiles_per_batch == 0, first, x)
    xn = _rmsnorm(x, g_ref[...]).astype(BF16)
    for c0 in range(0, D_FF, FF_CHUNK):
        c1 = min(c0 + FF_CHUNK, D_FF)
        gate = jnp.dot(xn, wg_ref[:, c0:c1], preferred_element_type=F32)
        up = jnp.dot(xn, wu_ref[:, c0:c1], preferred_element_type=F32)
        act_ref[:, c0:c1] = (gate * _sigmoid(gate) * up).astype(BF16)
    y = x + 0.5 * jnp.dot(act_ref[...], wd_ref[...], preferred_element_type=F32)
    if final:
        y = _rmsnorm(y, fg_ref[...])
    o_ref[...] = y


def _ffn(h, g, wg, wu, wd, final_g=None, *, tm, frames=None, head=None):
    rows = h.shape[0]
    final = final_g is not None
    row_spec = pl.BlockSpec((tm, D_MODEL), lambda i: (i, 0))
    in_spec = row_spec
    tiles_per_batch = None
    if frames is not None:
        lp, seq = frames
        if final:
            tiles_per_batch = seq // tm
            rows = rows // lp * seq
            start = lambda i: i // tiles_per_batch * lp + (lp - seq) + i % tiles_per_batch * tm
        else:
            tiles_per_batch = lp // tm
            rows = rows // seq * lp
            start = lambda i: (i // tiles_per_batch * seq
                               + jnp.maximum(i % tiles_per_batch * tm - (lp - seq), 0))
        in_spec = pl.BlockSpec((pl.Element(tm), pl.Element(D_MODEL)),
                               lambda i: (pl.multiple_of(start(i), F32_SUBLANES), 0))
    in_specs = [in_spec, _resident((1, D_MODEL)), _resident((D_MODEL, D_FF)),
                _resident((D_MODEL, D_FF)), _resident((D_FF, D_MODEL))]
    args = [h, g, wg, wu, wd]
    if final:
        in_specs.append(_resident((1, D_MODEL)))
        args.append(final_g)
    elif head is not None:
        in_specs.append(_resident(head.shape))
        args.append(head)
    return pl.pallas_call(
        functools.partial(_ffn_body, final=final, tiles_per_batch=tiles_per_batch),
        grid=(rows // tm,),
        in_specs=in_specs,
        out_specs=row_spec,
        out_shape=jax.ShapeDtypeStruct((rows, D_MODEL), F32),
        scratch_shapes=[pltpu.VMEM((tm, D_FF), BF16)],
        compiler_params=_params(1),
        name="ffn_final" if final else "ffn",
    )(*args)


def _proj_body(h_ref, g_ref, w_ref, q_ref, k_ref, v_ref, u_ref, *, tm, pad):
    xn = _rmsnorm(h_ref[...], g_ref[...]).astype(BF16)

    def col(i):
        return jnp.dot(xn, w_ref[:, i * SB_WIDTH:(i + 1) * SB_WIDTH], preferred_element_type=F32)

    q_ref[...] = (col(0) * (-1.0 / math.sqrt(SB_HEAD_DIM))).astype(BF16)
    k_ref[...] = col(1).astype(BF16)
    v_ref[...] = col(2).astype(BF16)
    u = col(3) * _sigmoid(col(4))
    pos = pl.program_id(1) * tm + lax.broadcasted_iota(jnp.int32, (tm, 1), 0)
    u_ref[...] = jnp.where(pos >= pad, u, 0.0)


def _proj(h, g, w, *, batch, lp, pad, tm):
    rows = h.shape[0]
    nt = lp // tm
    row = lambda width: pl.BlockSpec((tm, width), lambda b, j: (b * nt + j, 0))
    bf = jax.ShapeDtypeStruct((rows, SB_WIDTH), BF16)
    return pl.pallas_call(
        functools.partial(_proj_body, tm=tm, pad=pad),
        grid=(batch, nt),
        in_specs=[row(D_MODEL), _resident((1, D_MODEL)), _resident(w.shape)],
        out_specs=[row(SB_WIDTH), row(SB_WIDTH), row(SB_WIDTH), row(CONV_CH)],
        out_shape=[bf, bf, bf, jax.ShapeDtypeStruct((rows, CONV_CH), F32)],
        compiler_params=_params(2),
        name="mixer_proj",
    )(h, g, w)


def _attn_body(q_ref, k_ref, v_ref, o_ref, acc_ref, carry_ref, *, pad):
    blk = ATT_BLOCK
    i = pl.program_id(1)
    row = lax.broadcasted_iota(jnp.int32, (blk, blk), 0)
    lane = lax.broadcasted_iota(jnp.int32, (blk, blk), 1)
    qpos = i * blk + row
    half = jnp.concatenate([(row > lane).astype(BF16), jnp.ones((blk, blk), BF16)], axis=1)
    suffix_and_total = jnp.concatenate([half, half], axis=0)
    first = (lane < SB_HEAD_DIM).astype(BF16)
    second = (lane >= SB_HEAD_DIM).astype(BF16)
    nt = (((1,), (1,)), ((), ()))

    pair_cols = [slice(m * HEAD_PAIR, (m + 1) * HEAD_PAIR) for m in range(SB_HEADS // 2)]

    def sweep(blocks, carry, acc):
        masked, k_diag, v_diag = [], [], []
        for j in blocks:
            start = pl.multiple_of(jnp.maximum(j, 0) * blk, blk)
            kpos = start + lane
            valid = (kpos < qpos) & (kpos >= pad) & (j >= 0)
            masked.append(jnp.where(valid, 0.0, MASKED))
            kt = k_ref[pl.ds(start, blk), :]
            vt = v_ref[pl.ds(start, blk), :]
            k_diag.append([jnp.concatenate([kt[:, c] * first, kt[:, c] * second], axis=0)
                           for c in pair_cols])
            v_diag.append([jnp.concatenate([vt[:, c] * first, vt[:, c] * second], axis=0)
                           for c in pair_cols])
        zneg = [[lax.dot_general(q_ref[:, c], kd, nt, preferred_element_type=F32)
                 for c, kd in zip(pair_cols, kds)] for kds in k_diag]
        log_beta, split = [], []
        for b in range(len(blocks)):
            for h in range(SB_HEADS):
                zn = zneg[b][h // 2][:, (h % 2) * blk:(h % 2 + 1) * blk] + masked[b]
                lk = jnp.minimum(zn, 0.0) - jnp.log(1.0 + jnp.exp2(jnp.abs(zn) * (-LOG2E)))
                hi = lk.astype(BF16)
                lo = (lk - hi.astype(F32)).astype(BF16)
                log_beta.append(lk - zn)
                split.append(jnp.concatenate([hi, lo], axis=1))
        sums = jnp.dot(jnp.concatenate(split, axis=0), suffix_and_total,
                       preferred_element_type=F32)
        carry, acc = list(carry), list(acc)
        for b in range(len(blocks)):
            probs = []
            for h in range(SB_HEADS):
                r0 = (b * SB_HEADS + h) * blk
                after = sums[r0:r0 + blk, :blk]
                total = sums[r0:r0 + blk, blk:]
                if carry[h] is not None:
                    after = after + carry[h]
                probs.append(jnp.exp(log_beta[b * SB_HEADS + h] + after).astype(BF16))
                carry[h] = total if carry[h] is None else carry[h] + total
            for m in range(SB_HEADS // 2):
                out = jnp.dot(jnp.concatenate(probs[2 * m:2 * m + 2], axis=1), v_diag[b][m],
                              preferred_element_type=F32)
                acc[m] = out if acc[m] is None else acc[m] + out
        return carry, acc

    def store_state(carry, acc):
        for h in range(SB_HEADS):
            carry_ref[h] = carry[h]
        for m, c in enumerate(pair_cols):
            acc_ref[:, c] = acc[m]
        return (jnp.max(functools.reduce(jnp.maximum, carry)) > SKIP_LOG).astype(jnp.int32)

    live = store_state(*sweep([i - d for d in range(ATT_FIXED)],
                              [None] * SB_HEADS, [None] * (SB_HEADS // 2)))

    def body(state):
        j, _ = state
        carry, acc = sweep([j], [carry_ref[h] for h in range(SB_HEADS)],
                           [acc_ref[:, c] for c in pair_cols])
        return j - 1, store_state(carry, acc)

    lax.while_loop(lambda s: (s[0] >= 0) & (s[1] > 0), body, (i - ATT_FIXED, live))
    o_ref[...] = acc_ref[...].astype(BF16)


def _attention(q, k, v, *, batch, lp, pad):
    nq = lp // ATT_BLOCK
    qspec = pl.BlockSpec((ATT_BLOCK, SB_WIDTH), lambda b, i: (b * nq + i, 0))
    kvspec = pl.BlockSpec((lp, SB_WIDTH), lambda b, i: (b, 0))
    return pl.pallas_call(
        functools.partial(_attn_body, pad=pad),
        grid=(batch, nq),
        in_specs=[qspec, kvspec, kvspec],
        out_specs=qspec,
        out_shape=jax.ShapeDtypeStruct(q.shape, BF16),
        scratch_shapes=[pltpu.VMEM((ATT_BLOCK, SB_WIDTH), F32),
                        pltpu.VMEM((SB_HEADS, ATT_BLOCK, ATT_BLOCK), F32)],
        compiler_params=_params(2),
        name="sb_attention",
    )(q, k, v)


def _mix_body(h_ref, g_ref, wgate_ref, a_ref, u_ref, uprev_ref, cw_ref, cb_ref, lng_ref, lnb_ref,
              wao_ref, wco_ref, wo_ref, o_ref, ubuf_ref, shift_ref, conv_ref, *, tm):
    ubuf_ref[0:CONV_HALO, :] = uprev_ref[...]
    ubuf_ref[CONV_HALO:, :] = u_ref[...]
    base = CONV_HALO - (CONV_K - 1)
    sub = F32_SUBLANES
    for r in range(sub):
        n = tm + sub * ((CONV_K - 1 - r) // sub)
        shift_ref[r, 0:n // sub] = ubuf_ref[base + r:base + r + n, :].reshape(n // sub, sub, CONV_CH)

    def conv_chunk(c0):
        y = jnp.broadcast_to(cb_ref[...], (CONV_CHUNK // sub, sub, CONV_CH))
        for k in range(CONV_K):
            g0 = (c0 + k - k % sub) // sub
            y = y + cw_ref[k] * shift_ref[k % sub, g0:g0 + CONV_CHUNK // sub]
        y = y.reshape(CONV_CHUNK, CONV_CH)
        mu = jnp.mean(y, axis=-1, keepdims=True)
        yc = y - mu
        var = jnp.mean(yc * yc, axis=-1, keepdims=True)
        ln = yc * lax.rsqrt(var + LN_EPS) * lng_ref[...] + lnb_ref[...]
        conv_ref[c0:c0 + CONV_CHUNK, :] = (ln * _sigmoid(ln)).astype(BF16)

    for r0 in range(0, tm, MIX_ROWS):
        rows = slice(r0, r0 + MIX_ROWS)
        for c0 in range(r0, r0 + MIX_ROWS, CONV_CHUNK):
            conv_chunk(c0)
        x = h_ref[rows, :]
        xn = _rmsnorm(x, g_ref[...]).astype(BF16)
        gate_sb = _sigmoid(jnp.dot(xn, wgate_ref[:, :D_MODEL], preferred_element_type=F32))
        gate_cv = _sigmoid(jnp.dot(xn, wgate_ref[:, D_MODEL:], preferred_element_type=F32))
        y_sb = jnp.dot(a_ref[rows, :], wao_ref[...], preferred_element_type=F32)
        y_cv = jnp.dot(conv_ref[rows, :], wco_ref[...], preferred_element_type=F32)
        m = (gate_sb * y_sb + gate_cv * y_cv).astype(BF16)
        o_ref[rows, :] = x + jnp.dot(m, wo_ref[...], preferred_element_type=F32)


def _mix(h, g, wgate, a, u, cw, cb, lng, lnb, wao, wco, wo, *, tm):
    rows = h.shape[0]
    halo_per_tile = tm // CONV_HALO
    row = lambda width: pl.BlockSpec((tm, width), lambda i: (i, 0))
    prev = pl.BlockSpec((CONV_HALO, CONV_CH), lambda i: (jnp.maximum(i * halo_per_tile - 1, 0), 0))
    shift_rows = tm + F32_SUBLANES * ((CONV_K - 1) // F32_SUBLANES)
    return pl.pallas_call(
        functools.partial(_mix_body, tm=tm),
        grid=(rows // tm,),
        in_specs=[row(D_MODEL), _resident((1, D_MODEL)), _resident(wgate.shape), row(SB_WIDTH),
                  row(CONV_CH), prev, _resident(cw.shape), _resident((1, CONV_CH)),
                  _resident((1, CONV_CH)), _resident((1, CONV_CH)), _resident(wao.shape),
                  _resident(wco.shape), _resident(wo.shape)],
        out_specs=row(D_MODEL),
        out_shape=jax.ShapeDtypeStruct((rows, D_MODEL), F32),
        scratch_shapes=[pltpu.VMEM((CONV_HALO + tm, CONV_CH), F32),
                        pltpu.VMEM((F32_SUBLANES, shift_rows // F32_SUBLANES, F32_SUBLANES, CONV_CH), F32),
                        pltpu.VMEM((tm, CONV_CH), BF16)],
        compiler_params=_params(1),
        name="mixer_out",
    )(h, g, wgate, a, u, u, cw, cb, lng, lnb, wao, wco, wo)


def kernel(x, meta, ffn1_norm, ffn1_w_gate, ffn1_w_up, ffn1_w_down, mix_norm, w_in, w_attn_out,
           conv_w, conv_b, conv_ln_g, conv_ln_b, w_conv_out, w_out, ffn2_norm, ffn2_w_gate,
           ffn2_w_up, ffn2_w_down, final_norm):
    batch, seq, d = x.shape
    depth = w_in.shape[0]
    assert d == D_MODEL and meta.shape == (N_META, D_MODEL) and seq % ATT_BLOCK == 0
    assert N_META <= ATT_BLOCK
    pad = ATT_BLOCK - N_META
    lp = seq + ATT_BLOCK
    tm_ffn = 1040
    tm_mix = 640
    tm_out = 1024
    assert lp % tm_ffn == 0 and lp % tm_mix == 0 and tm_mix % CONV_HALO == 0
    assert tm_mix % CONV_CHUNK == 0 and seq % tm_out == 0

    head = jnp.concatenate([jnp.zeros((pad, d), x.dtype), meta.astype(x.dtype)], axis=0)
    h = x.reshape(batch * seq, d)

    bf = lambda w: w.astype(BF16)
    vec = lambda p: p.reshape(1, -1).astype(F32)
    conv_taps = lambda w: jnp.broadcast_to(w.astype(F32)[:, None, :], (CONV_K, F32_SUBLANES, CONV_CH))
    n_proj = 3 * SB_WIDTH + 2 * CONV_CH
    for l in range(depth):
        ends = dict(frames=(lp, seq), head=head) if l == 0 else {}
        h = _ffn(h, vec(ffn1_norm[l]), bf(ffn1_w_gate[l]), bf(ffn1_w_up[l]), bf(ffn1_w_down[l]),
                 tm=tm_ffn, **ends)
        q, k, v, u = _proj(h, vec(mix_norm[l]), bf(w_in[l, :, :n_proj]),
                           batch=batch, lp=lp, pad=pad, tm=tm_mix)
        a = _attention(q, k, v, batch=batch, lp=lp, pad=pad)
        h = _mix(h, vec(mix_norm[l]), bf(w_in[l, :, n_proj:]), a, u, conv_taps(conv_w[l]),
                 vec(conv_b[l]), vec(conv_ln_g[l]), vec(conv_ln_b[l]), bf(w_attn_out[l]),
                 bf(w_conv_out[l]), bf(w_out[l]), tm=tm_mix)
        ffn2 = (vec(ffn2_norm[l]), bf(ffn2_w_gate[l]), bf(ffn2_w_up[l]), bf(ffn2_w_down[l]))
        if l < depth - 1:
            h = _ffn(h, *ffn2, tm=tm_ffn)
    out = _ffn(h, *ffn2, vec(final_norm), tm=tm_out, frames=(lp, seq))
    return out.reshape(batch, seq, d)
```

```python
import functools
import math

import jax
import jax.numpy as jnp
from jax import lax
from jax.experimental import pallas as pl
from jax.experimental.pallas import tpu as pltpu

D_MODEL = 1024
N_META = 16
SB_HEADS = 8
SB_HEAD_DIM = 64
SB_WIDTH = SB_HEADS * SB_HEAD_DIM
HEAD_PAIR = 2 * SB_HEAD_DIM
CONV_CH = 512
CONV_K = 31
D_FF = 2816
NORM_EPS = 1e-6
LN_EPS = 1e-5
LOG2E = 1.4426950408889634

F32_SUBLANES = 8
ATT_BLOCK = 128
ATT_FIXED = 3
MXU_COLS = 256
FF_CHUNK = MXU_COLS
CONV_HALO = 32
CONV_CHUNK = 32
MIX_ROWS = 320
CAST_ROWS = 256
VMEM_LIMIT = 56 * 1024 * 1024

SKIP_LOG = -104.0
MASKED = 1e30

F32 = jnp.float32
BF16 = jnp.bfloat16


def _rmsnorm(x, g):
    return x * lax.rsqrt(jnp.mean(x * x, axis=-1, keepdims=True) + NORM_EPS) * g


def _sigmoid(x):
    return 1.0 / (1.0 + jnp.exp(-x))


def _resident(shape):
    return pl.BlockSpec(shape, lambda *_: (0,) * len(shape), pipeline_mode=pl.Buffered(1))


def _layer(w, l):
    return pl.BlockSpec((None,) + w.shape[1:], lambda *_: (l, 0, 0), pipeline_mode=pl.Buffered(1))


def _params(n_axes):
    return pltpu.CompilerParams(dimension_semantics=("arbitrary",) * n_axes,
                                vmem_limit_bytes=VMEM_LIMIT)


def _cast_body(w_ref, *o_refs, splits):
    c0 = 0
    for o_ref, c1 in zip(o_refs, splits):
        o_ref[...] = w_ref[:, c0:c1].astype(BF16)
        c0 = c1


def _to_bf16(w, splits=None):
    depth, k, n = w.shape
    splits = tuple(splits or (n,))
    widths = [c1 - c0 for c0, c1 in zip((0,) + splits[:-1], splits)]
    assert splits[-1] == n and k % CAST_ROWS == 0
    out = pl.pallas_call(
        functools.partial(_cast_body, splits=splits),
        grid=(depth, k // CAST_ROWS),
        in_specs=[pl.BlockSpec((None, CAST_ROWS, n), lambda l, i: (l, i, 0))],
        out_specs=[pl.BlockSpec((None, CAST_ROWS, wd), lambda l, i: (l, i, 0)) for wd in widths],
        out_shape=[jax.ShapeDtypeStruct((depth, k, wd), BF16) for wd in widths],
        compiler_params=_params(2),
        name="to_bf16",
    )(w)
    return out if len(out) > 1 else out[0]


def _ffn_body(h_ref, g_ref, wg_ref, wu_ref, wd_ref, *rest, final, tiles_per_batch):
    if final:
        fg_ref, o_ref, act_ref = rest
    elif tiles_per_batch:
        head_ref, o_ref, act_ref = rest
    else:
        o_ref, act_ref = rest
    x = h_ref[...]
    if tiles_per_batch and not final:
        n_head = head_ref.shape[0]
        first = jnp.concatenate([head_ref[...], x[:x.shape[0] - n_head]], axis=0)
        x = jnp.where(pl.program_id(0) % tiles_per_batch == 0, first, x)
    xn = _rmsnorm(x, g_ref[...]).astype(BF16)
    for c0 in range(0, D_FF, FF_CHUNK):
        c1 = min(c0 + FF_CHUNK, D_FF)
        gate = jnp.dot(xn, wg_ref[:, c0:c1], preferred_element_type=F32)
        up = jnp.dot(xn, wu_ref[:, c0:c1], preferred_element_type=F32)
        act_ref[:, c0:c1] = (gate * _sigmoid(gate) * up).astype(BF16)
    y = x + 0.5 * jnp.dot(act_ref[...], wd_ref[...], preferred_element_type=F32)
    if final:
        y = _rmsnorm(y, fg_ref[...])
    o_ref[...] = y


def _ffn(h, g, wg, wu, wd, l, final_g=None, *, tm, frames=None, head=None):
    rows = h.shape[0]
    final = final_g is not None
    row_spec = pl.BlockSpec((tm, D_MODEL), lambda i: (i, 0))
    in_spec = row_spec
    tiles_per_batch = None
    if frames is not None:
        lp, seq = frames
        if final:
            tiles_per_batch = seq // tm
            rows = rows // lp * seq
            start = lambda i: i // tiles_per_batch * lp + (lp - seq) + i % tiles_per_batch * tm
        else:
            tiles_per_batch = lp // tm
            rows = rows // seq * lp
            start = lambda i: (i // tiles_per_batch * seq
                               + jnp.maximum(i % tiles_per_batch * tm - (lp - seq), 0))
        in_spec = pl.BlockSpec((pl.Element(tm), pl.Element(D_MODEL)),
                               lambda i: (pl.multiple_of(start(i), F32_SUBLANES), 0))
    in_specs = [in_spec, _resident((1, D_MODEL)), _layer(wg, l), _layer(wu, l), _layer(wd, l)]
    args = [h, g, wg, wu, wd]
    if final:
        in_specs.append(_resident((1, D_MODEL)))
        args.append(final_g)
    elif head is not None:
        in_specs.append(_resident(head.shape))
        args.append(head)
    return pl.pallas_call(
        functools.partial(_ffn_body, final=final, tiles_per_batch=tiles_per_batch),
        grid=(rows // tm,),
        in_specs=in_specs,
        out_specs=row_spec,
        out_shape=jax.ShapeDtypeStruct((rows, D_MODEL), F32),
        scratch_shapes=[pltpu.VMEM((tm, D_FF), BF16)],
        compiler_params=_params(1),
        name="ffn_final" if final else "ffn",
    )(*args)


def _proj_body(h_ref, g_ref, w_ref, q_ref, k_ref, v_ref, u_ref, *, tm, pad):
    xn = _rmsnorm(h_ref[...], g_ref[...]).astype(BF16)

    def col(i):
        return jnp.dot(xn, w_ref[:, i * SB_WIDTH:(i + 1) * SB_WIDTH], preferred_element_type=F32)

    q_ref[...] = (col(0) * (-1.0 / math.sqrt(SB_HEAD_DIM))).astype(BF16)
    k_ref[...] = col(1).astype(BF16)
    v_ref[...] = col(2).astype(BF16)
    u = col(3) * _sigmoid(col(4))
    pos = pl.program_id(1) * tm + lax.broadcasted_iota(jnp.int32, (tm, 1), 0)
    u_ref[...] = jnp.where(pos >= pad, u, 0.0)


def _proj(h, g, w, l, *, batch, lp, pad, tm):
    rows = h.shape[0]
    nt = lp // tm
    row = lambda width: pl.BlockSpec((tm, width), lambda b, j: (b * nt + j, 0))
    bf = jax.ShapeDtypeStruct((rows, SB_WIDTH), BF16)
    return pl.pallas_call(
        functools.partial(_proj_body, tm=tm, pad=pad),
        grid=(batch, nt),
        in_specs=[row(D_MODEL), _resident((1, D_MODEL)), _layer(w, l)],
        out_specs=[row(SB_WIDTH), row(SB_WIDTH), row(SB_WIDTH), row(CONV_CH)],
        out_shape=[bf, bf, bf, jax.ShapeDtypeStruct((rows, CONV_CH), F32)],
        compiler_params=_params(2),
        name="mixer_proj",
    )(h, g, w)


def _attn_body(q_ref, k_ref, v_ref, o_ref, acc_ref, carry_ref, *, pad):
    blk = ATT_BLOCK
    i = pl.program_id(1)
    row = lax.broadcasted_iota(jnp.int32, (blk, blk), 0)
    lane = lax.broadcasted_iota(jnp.int32, (blk, blk), 1)
    qpos = i * blk + row
    half = jnp.concatenate([(row > lane).astype(BF16), jnp.ones((blk, blk), BF16)], axis=1)
    suffix_and_total = jnp.concatenate([half, half], axis=0)
    first = (lane < SB_HEAD_DIM).astype(BF16)
    second = (lane >= SB_HEAD_DIM).astype(BF16)
    nt = (((1,), (1,)), ((), ()))

    pair_cols = [slice(m * HEAD_PAIR, (m + 1) * HEAD_PAIR) for m in range(SB_HEADS // 2)]

    def sweep(blocks, carry, acc):
        masked, k_diag, v_diag = [], [], []
        for j, nr in blocks:
            start = pl.multiple_of(jnp.maximum(j, 0) * blk, blk)
            kpos = start + lane
            valid = (kpos < qpos) & (kpos >= pad) & (j >= 0)
            masked.append(jnp.where(valid, 0.0, MASKED)[:nr])
            kt = k_ref[pl.ds(start, blk), :]
            vt = v_ref[pl.ds(start, blk), :]
            k_diag.append([jnp.concatenate([kt[:, c] * first, kt[:, c] * second], axis=0)
                           for c in pair_cols])
            v_diag.append([jnp.concatenate([vt[:, c] * first, vt[:, c] * second], axis=0)
                           for c in pair_cols])
        zneg = [[lax.dot_general(q_ref[0:nr, c], kd, nt, preferred_element_type=F32)
                 for c, kd in zip(pair_cols, kds)]
                for (_, nr), kds in zip(blocks, k_diag)]
        log_beta, split = [], []
        for b in range(len(blocks)):
            for h in range(SB_HEADS):
                zn = zneg[b][h // 2][:, (h % 2) * blk:(h % 2 + 1) * blk] + masked[b]
                lk = jnp.minimum(zn, 0.0) - jnp.log(1.0 + jnp.exp2(jnp.abs(zn) * (-LOG2E)))
                hi = lk.astype(BF16)
                lo = (lk - hi.astype(F32)).astype(BF16)
                log_beta.append(lk - zn)
                split.append(jnp.concatenate([hi, lo], axis=1))
        sums = jnp.dot(jnp.concatenate(split, axis=0), suffix_and_total,
                       preferred_element_type=F32)

        def add_rows(old, new):
            if old is None:
                assert new.shape[0] == blk
                return new
            nr = new.shape[0]
            top = old[:nr] + new
            return top if nr == blk else jnp.concatenate([top, old[nr:]], axis=0)

        carry, acc = list(carry), list(acc)
        states = []
        r0 = 0
        for b, (_, nr) in enumerate(blocks):
            probs = []
            for h in range(SB_HEADS):
                after = sums[r0:r0 + nr, :blk]
                total = sums[r0:r0 + nr, blk:]
                r0 += nr
                if carry[h] is not None:
                    after = after + carry[h][:nr]
                probs.append(jnp.exp(log_beta[b * SB_HEADS + h] + after).astype(BF16))
                carry[h] = add_rows(carry[h], total)
            for m in range(SB_HEADS // 2):
                out = jnp.dot(jnp.concatenate(probs[2 * m:2 * m + 2], axis=1), v_diag[b][m],
                              preferred_element_type=F32)
                acc[m] = add_rows(acc[m], out)
            states.append((list(carry), list(acc)))
        return states

    def store_state(carry, acc):
        for h in range(SB_HEADS):
            carry_ref[h] = carry[h]
        for m, c in enumerate(pair_cols):
            acc_ref[:, c] = acc[m]

    def is_live(carry):
        return (jnp.max(functools.reduce(jnp.maximum, carry)) > SKIP_LOG).astype(jnp.int32)

    fixed = [(i - d, blk) for d in range(ATT_FIXED - 1)] + [(i - (ATT_FIXED - 1), blk // 2)]
    states = sweep(fixed, [None] * SB_HEADS, [None] * (SB_HEADS // 2))
    store_state(*states[-2])
    carry, acc = states[-1]
    live = is_live(carry)
    o_ref[...] = jnp.concatenate(acc, axis=1).astype(BF16)

    def body(state):
        j, _ = state
        carry, acc = sweep([(j, blk)], [carry_ref[h] for h in range(SB_HEADS)],
                           [acc_ref[:, c] for c in pair_cols])[-1]
        store_state(carry, acc)
        return j - 1, is_live(carry)

    lax.while_loop(lambda s: (s[0] >= 0) & (s[1] > 0), body, (i - (ATT_FIXED - 1), live))

    @pl.when(live > 0)
    def _():
        o_ref[...] = acc_ref[...].astype(BF16)


def _attention(q, k, v, *, batch, lp, pad):
    nq = lp // ATT_BLOCK
    qspec = pl.BlockSpec((ATT_BLOCK, SB_WIDTH), lambda b, i: (b * nq + i, 0))
    kvspec = pl.BlockSpec((lp, SB_WIDTH), lambda b, i: (b, 0))
    return pl.pallas_call(
        functools.partial(_attn_body, pad=pad),
        grid=(batch, nq),
        in_specs=[qspec, kvspec, kvspec],
        out_specs=qspec,
        out_shape=jax.ShapeDtypeStruct(q.shape, BF16),
        scratch_shapes=[pltpu.VMEM((ATT_BLOCK, SB_WIDTH), F32),
                        pltpu.VMEM((SB_HEADS, ATT_BLOCK, ATT_BLOCK), F32)],
        compiler_params=_params(2),
        name="sb_attention",
    )(q, k, v)


def _mix_body(h_ref, g_ref, wgate_ref, a_ref, u_ref, uprev_ref, cw_ref, cb_ref, lng_ref, lnb_ref,
              wao_ref, wco_ref, wo_ref, o_ref, ubuf_ref, shift_ref, conv_ref, *, tm):
    ubuf_ref[0:CONV_HALO, :] = uprev_ref[...]
    ubuf_ref[CONV_HALO:, :] = u_ref[...]
    base = CONV_HALO - (CONV_K - 1)
    sub = F32_SUBLANES
    for r in range(sub):
        n = tm + sub * ((CONV_K - 1 - r) // sub)
        shift_ref[r, 0:n // sub] = ubuf_ref[base + r:base + r + n, :].reshape(n // sub, sub, CONV_CH)

    def conv_chunk(c0):
        y = jnp.broadcast_to(cb_ref[...], (CONV_CHUNK // sub, sub, CONV_CH))
        for k in range(CONV_K):
            g0 = (c0 + k - k % sub) // sub
            y = y + cw_ref[k] * shift_ref[k % sub, g0:g0 + CONV_CHUNK // sub]
        y = y.reshape(CONV_CHUNK, CONV_CH)
        mu = jnp.mean(y, axis=-1, keepdims=True)
        yc = y - mu
        var = jnp.mean(yc * yc, axis=-1, keepdims=True)
        ln = yc * lax.rsqrt(var + LN_EPS) * lng_ref[...] + lnb_ref[...]
        conv_ref[c0:c0 + CONV_CHUNK, :] = (ln * _sigmoid(ln)).astype(BF16)

    for r0 in range(0, tm, MIX_ROWS):
        rows = slice(r0, r0 + MIX_ROWS)
        for c0 in range(r0, r0 + MIX_ROWS, CONV_CHUNK):
            conv_chunk(c0)
        x = h_ref[rows, :]
        xn = _rmsnorm(x, g_ref[...]).astype(BF16)
        gate_sb = _sigmoid(jnp.dot(xn, wgate_ref[:, :D_MODEL], preferred_element_type=F32))
        gate_cv = _sigmoid(jnp.dot(xn, wgate_ref[:, D_MODEL:], preferred_element_type=F32))
        y_sb = jnp.dot(a_ref[rows, :], wao_ref[...], preferred_element_type=F32)
        y_cv = jnp.dot(conv_ref[rows, :], wco_ref[...], preferred_element_type=F32)
        m = (gate_sb * y_sb + gate_cv * y_cv).astype(BF16)
        o_ref[rows, :] = x + jnp.dot(m, wo_ref[...], preferred_element_type=F32)


def _mix(h, g, wgate, a, u, cw, cb, lng, lnb, wao, wco, wo, l, *, tm):
    rows = h.shape[0]
    halo_per_tile = tm // CONV_HALO
    row = lambda width: pl.BlockSpec((tm, width), lambda i: (i, 0))
    prev = pl.BlockSpec((CONV_HALO, CONV_CH), lambda i: (jnp.maximum(i * halo_per_tile - 1, 0), 0))
    shift_rows = tm + F32_SUBLANES * ((CONV_K - 1) // F32_SUBLANES)
    return pl.pallas_call(
        functools.partial(_mix_body, tm=tm),
        grid=(rows // tm,),
        in_specs=[row(D_MODEL), _resident((1, D_MODEL)), _layer(wgate, l), row(SB_WIDTH),
                  row(CONV_CH), prev, _resident(cw.shape), _resident((1, CONV_CH)),
                  _resident((1, CONV_CH)), _resident((1, CONV_CH)), _layer(wao, l),
                  _layer(wco, l), _layer(wo, l)],
        out_specs=row(D_MODEL),
        out_shape=jax.ShapeDtypeStruct((rows, D_MODEL), F32),
        scratch_shapes=[pltpu.VMEM((CONV_HALO + tm, CONV_CH), F32),
                        pltpu.VMEM((F32_SUBLANES, shift_rows // F32_SUBLANES, F32_SUBLANES, CONV_CH), F32),
                        pltpu.VMEM((tm, CONV_CH), BF16)],
        compiler_params=_params(1),
        name="mixer_out",
    )(h, g, wgate, a, u, u, cw, cb, lng, lnb, wao, wco, wo)


def kernel(x, meta, ffn1_norm, ffn1_w_gate, ffn1_w_up, ffn1_w_down, mix_norm, w_in, w_attn_out,
           conv_w, conv_b, conv_ln_g, conv_ln_b, w_conv_out, w_out, ffn2_norm, ffn2_w_gate,
           ffn2_w_up, ffn2_w_down, final_norm):
    batch, seq, d = x.shape
    depth = w_in.shape[0]
    assert d == D_MODEL and meta.shape == (N_META, D_MODEL) and seq % ATT_BLOCK == 0
    assert N_META <= ATT_BLOCK
    pad = ATT_BLOCK - N_META
    lp = seq + ATT_BLOCK
    tm_ffn = 1040
    tm_mix = 640
    tm_out = 1024
    assert lp % tm_ffn == 0 and lp % tm_mix == 0 and tm_mix % CONV_HALO == 0
    assert tm_mix % CONV_CHUNK == 0 and seq % tm_out == 0

    head = jnp.concatenate([jnp.zeros((pad, d), x.dtype), meta.astype(x.dtype)], axis=0)
    h = x.reshape(batch * seq, d)

    vec = lambda p: p.reshape(1, -1).astype(F32)
    conv_taps = lambda w: jnp.broadcast_to(w.astype(F32)[:, None, :], (CONV_K, F32_SUBLANES, CONV_CH))
    ffn1 = [_to_bf16(w) for w in (ffn1_w_gate, ffn1_w_up, ffn1_w_down)]
    ffn2 = [_to_bf16(w) for w in (ffn2_w_gate, ffn2_w_up, ffn2_w_down)]
    w_proj, w_gates = _to_bf16(w_in, (3 * SB_WIDTH + 2 * CONV_CH, w_in.shape[2]))
    w_ao, w_co, w_o = _to_bf16(w_attn_out), _to_bf16(w_conv_out), _to_bf16(w_out)
    for l in range(depth):
        ends = dict(frames=(lp, seq), head=head) if l == 0 else {}
        h = _ffn(h, vec(ffn1_norm[l]), *ffn1, l, tm=tm_ffn, **ends)
        q, k, v, u = _proj(h, vec(mix_norm[l]), w_proj, l, batch=batch, lp=lp, pad=pad, tm=tm_mix)
        a = _attention(q, k, v, batch=batch, lp=lp, pad=pad)
        h = _mix(h, vec(mix_norm[l]), w_gates, a, u, conv_taps(conv_w[l]), vec(conv_b[l]),
                 vec(conv_ln_g[l]), vec(conv_ln_b[l]), w_ao, w_co, w_o, l, tm=tm_mix)
        if l < depth - 1:
            h = _ffn(h, vec(ffn2_norm[l]), *ffn2, l, tm=tm_ffn)
    out = _ffn(h, vec(ffn2_norm[depth - 1]), *ffn2, depth - 1, vec(final_norm), tm=tm_out,
               frames=(lp, seq))
    return out.reshape(batch, seq, d)
```

```python
import functools
import math

import jax
import jax.numpy as jnp
from jax import lax
from jax.experimental import pallas as pl
from jax.experimental.pallas import tpu as pltpu

D_MODEL = 1024
N_META = 16
SB_HEADS = 8
SB_HEAD_DIM = 64
SB_WIDTH = SB_HEADS * SB_HEAD_DIM
HEAD_PAIR = 2 * SB_HEAD_DIM
CONV_CH = 512
CONV_K = 31
D_FF = 2816
NORM_EPS = 1e-6
LN_EPS = 1e-5
LOG2E = 1.4426950408889634

F32_SUBLANES = 8
ATT_BLOCK = 128
ATT_HEAD_GROUPS = 2
ATT_FIXED = 3
MXU_COLS = 256
FF_CHUNK = MXU_COLS
CONV_HALO = 32
CONV_CHUNK = 32
MIX_ROWS = 320
CAST_ROWS = 128
VMEM_LIMIT = 56 * 1024 * 1024

SKIP_LOG = -104.0
MASKED = 1e30

F32 = jnp.float32
BF16 = jnp.bfloat16


def _rmsnorm(x, g):
    return x * lax.rsqrt(jnp.mean(x * x, axis=-1, keepdims=True) + NORM_EPS) * g


def _sigmoid(x):
    return 1.0 / (1.0 + jnp.exp(-x))


def _resident(shape):
    return pl.BlockSpec(shape, lambda *_: (0,) * len(shape), pipeline_mode=pl.Buffered(1))


def _layer(w, l):
    return pl.BlockSpec((None,) + w.shape[1:], lambda *_: (l, 0, 0), pipeline_mode=pl.Buffered(1))


def _params(n_axes):
    return pltpu.CompilerParams(dimension_semantics=("arbitrary",) * n_axes,
                                vmem_limit_bytes=VMEM_LIMIT)


def _cast_body(w_ref, *o_refs, splits):
    c0 = 0
    for o_ref, c1 in zip(o_refs, splits):
        o_ref[...] = w_ref[:, c0:c1].astype(BF16)
        c0 = c1


def _to_bf16(w, splits=None):
    depth, k, n = w.shape
    splits = tuple(splits or (n,))
    widths = [c1 - c0 for c0, c1 in zip((0,) + splits[:-1], splits)]
    assert splits[-1] == n and k % CAST_ROWS == 0
    out = pl.pallas_call(
        functools.partial(_cast_body, splits=splits),
        grid=(depth, k // CAST_ROWS),
        in_specs=[pl.BlockSpec((None, CAST_ROWS, n), lambda l, i: (l, i, 0))],
        out_specs=[pl.BlockSpec((None, CAST_ROWS, wd), lambda l, i: (l, i, 0)) for wd in widths],
        out_shape=[jax.ShapeDtypeStruct((depth, k, wd), BF16) for wd in widths],
        compiler_params=_params(2),
        name="to_bf16",
    )(w)
    return out if len(out) > 1 else out[0]


def _ffn_body(h_ref, g_ref, wg_ref, wu_ref, wd_ref, *rest, final, tiles_per_batch):
    if final:
        fg_ref, o_ref, act_ref = rest
    elif tiles_per_batch:
        head_ref, o_ref, act_ref = rest
    else:
        o_ref, act_ref = rest
    x = h_ref[...]
    if tiles_per_batch and not final:
        n_head = head_ref.shape[0]
        first = jnp.concatenate([head_ref[...], x[:x.shape[0] - n_head]], axis=0)
        x = jnp.where(pl.program_id(0) % tiles_per_batch == 0, first, x)
    xn = _rmsnorm(x, g_ref[...]).astype(BF16)
    for c0 in range(0, D_FF, FF_CHUNK):
        c1 = min(c0 + FF_CHUNK, D_FF)
        gate = jnp.dot(xn, wg_ref[:, c0:c1], preferred_element_type=F32)
        up = jnp.dot(xn, wu_ref[:, c0:c1], preferred_element_type=F32)
        act_ref[:, c0:c1] = (gate * _sigmoid(gate) * up).astype(BF16)
    y = x + 0.5 * jnp.dot(act_ref[...], wd_ref[...], preferred_element_type=F32)
    if final:
        y = _rmsnorm(y, fg_ref[...])
    o_ref[...] = y


def _ffn(h, g, wg, wu, wd, l, final_g=None, *, tm, frames=None, head=None):
    rows = h.shape[0]
    final = final_g is not None
    row_spec = pl.BlockSpec((tm, D_MODEL), lambda i: (i, 0))
    in_spec = row_spec
    tiles_per_batch = None
    if frames is not None:
        lp, seq = frames
        if final:
            tiles_per_batch = seq // tm
            rows = rows // lp * seq
            start = lambda i: i // tiles_per_batch * lp + (lp - seq) + i % tiles_per_batch * tm
        else:
            tiles_per_batch = lp // tm
            rows = rows // seq * lp
            start = lambda i: (i // tiles_per_batch * seq
                               + jnp.maximum(i % tiles_per_batch * tm - (lp - seq), 0))
        in_spec = pl.BlockSpec((pl.Element(tm), pl.Element(D_MODEL)),
                               lambda i: (pl.multiple_of(start(i), F32_SUBLANES), 0))
    in_specs = [in_spec, _resident((1, D_MODEL)), _layer(wg, l), _layer(wu, l), _layer(wd, l)]
    args = [h, g, wg, wu, wd]
    if final:
        in_specs.append(_resident((1, D_MODEL)))
        args.append(final_g)
    elif head is not None:
        in_specs.append(_resident(head.shape))
        args.append(head)
    return pl.pallas_call(
        functools.partial(_ffn_body, final=final, tiles_per_batch=tiles_per_batch),
        grid=(rows // tm,),
        in_specs=in_specs,
        out_specs=row_spec,
        out_shape=jax.ShapeDtypeStruct((rows, D_MODEL), F32),
        scratch_shapes=[pltpu.VMEM((tm, D_FF), BF16)],
        compiler_params=_params(1),
        name="ffn_final" if final else "ffn",
    )(*args)


def _proj_body(h_ref, g_ref, w_ref, q_ref, k_ref, v_ref, u_ref, *, tm, pad):
    xn = _rmsnorm(h_ref[...], g_ref[...]).astype(BF16)

    def col(i):
        return jnp.dot(xn, w_ref[:, i * SB_WIDTH:(i + 1) * SB_WIDTH], preferred_element_type=F32)

    q_ref[...] = (col(0) * (-1.0 / math.sqrt(SB_HEAD_DIM))).astype(BF16)
    k_ref[...] = col(1).astype(BF16)
    v_ref[...] = col(2).astype(BF16)
    u = col(3) * _sigmoid(col(4))
    pos = pl.program_id(1) * tm + lax.broadcasted_iota(jnp.int32, (tm, 1), 0)
    u_ref[...] = jnp.where(pos >= pad, u, 0.0)


def _proj(h, g, w, l, *, batch, lp, pad, tm):
    rows = h.shape[0]
    nt = lp // tm
    row = lambda width: pl.BlockSpec((tm, width), lambda b, j: (b * nt + j, 0))
    bf = jax.ShapeDtypeStruct((rows, SB_WIDTH), BF16)
    return pl.pallas_call(
        functools.partial(_proj_body, tm=tm, pad=pad),
        grid=(batch, nt),
        in_specs=[row(D_MODEL), _resident((1, D_MODEL)), _layer(w, l)],
        out_specs=[row(SB_WIDTH), row(SB_WIDTH), row(SB_WIDTH), row(CONV_CH)],
        out_shape=[bf, bf, bf, jax.ShapeDtypeStruct((rows, CONV_CH), F32)],
        compiler_params=_params(2),
        name="mixer_proj",
    )(h, g, w)


def _attn_body(q_ref, k_ref, v_ref, o_ref, acc_ref, carry_ref, *, pad):
    blk = ATT_BLOCK
    i = pl.program_id(1)
    row = lax.broadcasted_iota(jnp.int32, (blk, blk), 0)
    lane = lax.broadcasted_iota(jnp.int32, (blk, blk), 1)
    qpos = i * blk + row
    half = jnp.concatenate([(row > lane).astype(BF16), jnp.ones((blk, blk), BF16)], axis=1)
    suffix_and_total = jnp.concatenate([half, half], axis=0)
    first = (lane < SB_HEAD_DIM).astype(BF16)
    second = (lane >= SB_HEAD_DIM).astype(BF16)
    nt = (((1,), (1,)), ((), ()))

    pair_cols = [slice(m * HEAD_PAIR, (m + 1) * HEAD_PAIR) for m in range(SB_HEADS // 2)]
    head_groups = [range(g, g + SB_HEADS // ATT_HEAD_GROUPS)
                   for g in range(0, SB_HEADS, SB_HEADS // ATT_HEAD_GROUPS)]

    def sweep(blocks, carry, acc):
        masked, keys, values = [], [], []
        for j, nr in blocks:
            start = pl.multiple_of(jnp.maximum(j, 0) * blk, blk)
            kpos = start + lane
            valid = (kpos < qpos) & (kpos >= pad) & (j >= 0)
            masked.append(jnp.where(valid, 0.0, MASKED)[:nr])
            keys.append(k_ref[pl.ds(start, blk), :])
            values.append(v_ref[pl.ds(start, blk), :])
        q_pair = [(q_ref[:, c] * first, q_ref[:, c] * second) for c in pair_cols]
        zneg = [[lax.dot_general(jnp.concatenate([qa[:nr], qb[:nr]], axis=0), kt[:, c], nt,
                                 preferred_element_type=F32)
                 for c, (qa, qb) in zip(pair_cols, q_pair)]
                for (_, nr), kt in zip(blocks, keys)]
        log_beta, sums = {}, {}
        for heads in head_groups:
            split = []
            for b, (_, nr) in enumerate(blocks):
                for h in heads:
                    zn = zneg[b][h // 2][(h % 2) * nr:(h % 2 + 1) * nr] + masked[b]
                    lk = jnp.minimum(zn, 0.0) - jnp.log(1.0 + jnp.exp2(jnp.abs(zn) * (-LOG2E)))
                    hi = lk.astype(BF16)
                    lo = (lk - hi.astype(F32)).astype(BF16)
                    log_beta[b, h] = lk - zn
                    split.append(jnp.concatenate([hi, lo], axis=1))
            group_sums = jnp.dot(jnp.concatenate(split, axis=0), suffix_and_total,
                                 preferred_element_type=F32)
            r0 = 0
            for b, (_, nr) in enumerate(blocks):
                for h in heads:
                    sums[b, h] = group_sums[r0:r0 + nr]
                    r0 += nr

        def add_rows(old, new):
            if old is None:
                assert new.shape[0] == blk
                return new
            nr = new.shape[0]
            top = old[:nr] + new
            return top if nr == blk else jnp.concatenate([top, old[nr:]], axis=0)

        carry, acc = list(carry), list(acc)
        states = []
        for b, (_, nr) in enumerate(blocks):
            probs = []
            for h in range(SB_HEADS):
                after = sums[b, h][:, :blk]
                total = sums[b, h][:, blk:]
                if carry[h] is not None:
                    after = after + carry[h][:nr]
                probs.append(jnp.exp(log_beta[b, h] + after).astype(BF16))
                carry[h] = add_rows(carry[h], total)
            for m, c in enumerate(pair_cols):
                both = jnp.dot(jnp.concatenate(probs[2 * m:2 * m + 2], axis=0), values[b][:, c],
                               preferred_element_type=F32)
                own = lax.broadcasted_iota(jnp.int32, (nr, HEAD_PAIR), 1) < SB_HEAD_DIM
                acc[m] = add_rows(acc[m], jnp.where(own, both[:nr], both[nr:]))
            states.append((list(carry), list(acc)))
        return states

    def store_state(carry, acc):
        for h in range(SB_HEADS):
            carry_ref[h] = carry[h]
        for m, c in enumerate(pair_cols):
            acc_ref[:, c] = acc[m]

    def is_live(carry):
        return (jnp.max(functools.reduce(jnp.maximum, carry)) > SKIP_LOG).astype(jnp.int32)

    fixed = [(i - d, blk) for d in range(ATT_FIXED - 1)] + [(i - (ATT_FIXED - 1), blk // 2)]
    states = sweep(fixed, [None] * SB_HEADS, [None] * (SB_HEADS // 2))
    store_state(*states[-2])
    carry, acc = states[-1]
    live = is_live(carry)
    o_ref[...] = jnp.concatenate(acc, axis=1).astype(BF16)

    def body(state):
        j, _ = state
        carry, acc = sweep([(j, blk)], [carry_ref[h] for h in range(SB_HEADS)],
                           [acc_ref[:, c] for c in pair_cols])[-1]
        store_state(carry, acc)
        return j - 1, is_live(carry)

    lax.while_loop(lambda s: (s[0] >= 0) & (s[1] > 0), body, (i - (ATT_FIXED - 1), live))

    @pl.when(live > 0)
    def _():
        o_ref[...] = acc_ref[...].astype(BF16)


def _attention(q, k, v, *, batch, lp, pad):
    nq = lp // ATT_BLOCK
    qspec = pl.BlockSpec((ATT_BLOCK, SB_WIDTH), lambda b, i: (b * nq + i, 0))
    kvspec = pl.BlockSpec((lp, SB_WIDTH), lambda b, i: (b, 0))
    return pl.pallas_call(
        functools.partial(_attn_body, pad=pad),
        grid=(batch, nq),
        in_specs=[qspec, kvspec, kvspec],
        out_specs=qspec,
        out_shape=jax.ShapeDtypeStruct(q.shape, BF16),
        scratch_shapes=[pltpu.VMEM((ATT_BLOCK, SB_WIDTH), F32),
                        pltpu.VMEM((SB_HEADS, ATT_BLOCK, ATT_BLOCK), F32)],
        compiler_params=_params(2),
        name="sb_attention",
    )(q, k, v)


def _mix_body(h_ref, g_ref, wgate_ref, a_ref, u_ref, uprev_ref, cw_ref, cb_ref, lng_ref, lnb_ref,
              wao_ref, wco_ref, wo_ref, o_ref, ubuf_ref, shift_ref, conv_ref, *, tm):
    ubuf_ref[0:CONV_HALO, :] = uprev_ref[...]
    ubuf_ref[CONV_HALO:, :] = u_ref[...]
    base = CONV_HALO - (CONV_K - 1)
    sub = F32_SUBLANES
    for r in range(sub):
        n = tm + sub * ((CONV_K - 1 - r) // sub)
        shift_ref[r, 0:n // sub] = ubuf_ref[base + r:base + r + n, :].reshape(n // sub, sub, CONV_CH)

    def conv_chunk(c0):
        y = jnp.broadcast_to(cb_ref[...], (CONV_CHUNK // sub, sub, CONV_CH))
        for k in range(CONV_K):
            g0 = (c0 + k - k % sub) // sub
            y = y + cw_ref[k] * shift_ref[k % sub, g0:g0 + CONV_CHUNK // sub]
        y = y.reshape(CONV_CHUNK, CONV_CH)
        mu = jnp.mean(y, axis=-1, keepdims=True)
        yc = y - mu
        var = jnp.mean(yc * yc, axis=-1, keepdims=True)
        ln = yc * lax.rsqrt(var + LN_EPS) * lng_ref[...] + lnb_ref[...]
        conv_ref[c0:c0 + CONV_CHUNK, :] = (ln * _sigmoid(ln)).astype(BF16)

    for r0 in range(0, tm, MIX_ROWS):
        rows = slice(r0, r0 + MIX_ROWS)
        for c0 in range(r0, r0 + MIX_ROWS, CONV_CHUNK):
            conv_chunk(c0)
        x = h_ref[rows, :]
        xn = _rmsnorm(x, g_ref[...]).astype(BF16)
        gate_sb = _sigmoid(jnp.dot(xn, wgate_ref[:, :D_MODEL], preferred_element_type=F32))
        gate_cv = _sigmoid(jnp.dot(xn, wgate_ref[:, D_MODEL:], preferred_element_type=F32))
        y_sb = jnp.dot(a_ref[rows, :], wao_ref[...], preferred_element_type=F32)
        y_cv = jnp.dot(conv_ref[rows, :], wco_ref[...], preferred_element_type=F32)
        m = (gate_sb * y_sb + gate_cv * y_cv).astype(BF16)
        o_ref[rows, :] = x + jnp.dot(m, wo_ref[...], preferred_element_type=F32)


def _mix(h, g, wgate, a, u, cw, cb, lng, lnb, wao, wco, wo, l, *, tm):
    rows = h.shape[0]
    halo_per_tile = tm // CONV_HALO
    row = lambda width: pl.BlockSpec((tm, width), lambda i: (i, 0))
    prev = pl.BlockSpec((CONV_HALO, CONV_CH), lambda i: (jnp.maximum(i * halo_per_tile - 1, 0), 0))
    shift_rows = tm + F32_SUBLANES * ((CONV_K - 1) // F32_SUBLANES)
    return pl.pallas_call(
        functools.partial(_mix_body, tm=tm),
        grid=(rows // tm,),
        in_specs=[row(D_MODEL), _resident((1, D_MODEL)), _layer(wgate, l), row(SB_WIDTH),
                  row(CONV_CH), prev, _resident(cw.shape), _resident((1, CONV_CH)),
                  _resident((1, CONV_CH)), _resident((1, CONV_CH)), _layer(wao, l),
                  _layer(wco, l), _layer(wo, l)],
        out_specs=row(D_MODEL),
        out_shape=jax.ShapeDtypeStruct((rows, D_MODEL), F32),
        scratch_shapes=[pltpu.VMEM((CONV_HALO + tm, CONV_CH), F32),
                        pltpu.VMEM((F32_SUBLANES, shift_rows // F32_SUBLANES, F32_SUBLANES, CONV_CH), F32),
                        pltpu.VMEM((tm, CONV_CH), BF16)],
        compiler_params=_params(1),
        name="mixer_out",
    )(h, g, wgate, a, u, u, cw, cb, lng, lnb, wao, wco, wo)


def kernel(x, meta, ffn1_norm, ffn1_w_gate, ffn1_w_up, ffn1_w_down, mix_norm, w_in, w_attn_out,
           conv_w, conv_b, conv_ln_g, conv_ln_b, w_conv_out, w_out, ffn2_norm, ffn2_w_gate,
           ffn2_w_up, ffn2_w_down, final_norm):
    batch, seq, d = x.shape
    depth = w_in.shape[0]
    assert d == D_MODEL and meta.shape == (N_META, D_MODEL) and seq % ATT_BLOCK == 0
    assert N_META <= ATT_BLOCK
    pad = ATT_BLOCK - N_META
    lp = seq + ATT_BLOCK
    tm_ffn = 1040
    tm_mix = 640
    tm_out = 1024
    assert lp % tm_ffn == 0 and lp % tm_mix == 0 and tm_mix % CONV_HALO == 0
    assert tm_mix % CONV_CHUNK == 0 and seq % tm_out == 0

    head = jnp.concatenate([jnp.zeros((pad, d), x.dtype), meta.astype(x.dtype)], axis=0)
    h = x.reshape(batch * seq, d)

    vec = lambda p: p.reshape(1, -1).astype(F32)
    conv_taps = lambda w: jnp.broadcast_to(w.astype(F32)[:, None, :], (CONV_K, F32_SUBLANES, CONV_CH))
    ffn1 = [_to_bf16(w) for w in (ffn1_w_gate, ffn1_w_up, ffn1_w_down)]
    ffn2 = [_to_bf16(w) for w in (ffn2_w_gate, ffn2_w_up, ffn2_w_down)]
    w_proj, w_gates = _to_bf16(w_in, (3 * SB_WIDTH + 2 * CONV_CH, w_in.shape[2]))
    w_ao, w_co, w_o = _to_bf16(w_attn_out), _to_bf16(w_conv_out), _to_bf16(w_out)
    for l in range(depth):
        ends = dict(frames=(lp, seq), head=head) if l == 0 else {}
        h = _ffn(h, vec(ffn1_norm[l]), *ffn1, l, tm=tm_ffn, **ends)
        q, k, v, u = _proj(h, vec(mix_norm[l]), w_proj, l, batch=batch, lp=lp, pad=pad, tm=tm_mix)
        a = _attention(q, k, v, batch=batch, lp=lp, pad=pad)
        h = _mix(h, vec(mix_norm[l]), w_gates, a, u, conv_taps(conv_w[l]), vec(conv_b[l]),
                 vec(conv_ln_g[l]), vec(conv_ln_b[l]), w_ao, w_co, w_o, l, tm=tm_mix)
        if l < depth - 1:
            h = _ffn(h, vec(ffn2_norm[l]), *ffn2, l, tm=tm_ffn)
    out = _ffn(h, vec(ffn2_norm[depth - 1]), *ffn2, depth - 1, vec(final_norm), tm=tm_out,
               frames=(lp, seq))
    return out.reshape(batch, seq, d)
```

```python
import functools
import math

import jax
import jax.numpy as jnp
from jax import lax
from jax.experimental import pallas as pl
from jax.experimental.pallas import tpu as pltpu

D_MODEL = 1024
N_META = 16
SB_HEADS = 8
SB_HEAD_DIM = 64
SB_WIDTH = SB_HEADS * SB_HEAD_DIM
HEAD_PAIR = 2 * SB_HEAD_DIM
CONV_CH = 512
CONV_K = 31
D_FF = 2816
NORM_EPS = 1e-6
LN_EPS = 1e-5
LOG2E = 1.4426950408889634

F32_SUBLANES = 8
ATT_BLOCK = 128
ATT_HEAD_GROUPS = 2
ATT_FIXED = 3
MXU_COLS = 256
FF_CHUNK = MXU_COLS
CONV_HALO = 32
CONV_CHUNK = 32
MIX_ROWS = 320
CAST_STEPS = 2
VMEM_LIMIT = 56 * 1024 * 1024

SKIP_LOG = -104.0
MASKED = 1e30

F32 = jnp.float32
BF16 = jnp.bfloat16


def _rmsnorm(x, g):
    return x * lax.rsqrt(jnp.mean(x * x, axis=-1, keepdims=True) + NORM_EPS) * g


def _sigmoid(x):
    return 1.0 / (1.0 + jnp.exp(-x))


def _resident(shape):
    return pl.BlockSpec(shape, lambda *_: (0,) * len(shape), pipeline_mode=pl.Buffered(1))


def _layer(w, l):
    return pl.BlockSpec((None,) + w.shape[1:], lambda *_: (l, 0, 0), pipeline_mode=pl.Buffered(1))


def _params(n_axes):
    return pltpu.CompilerParams(dimension_semantics=("arbitrary",) * n_axes,
                                vmem_limit_bytes=VMEM_LIMIT)


def _cast_body(w_ref, *o_refs, splits):
    c0 = 0
    for o_ref, c1 in zip(o_refs, splits):
        o_ref[...] = w_ref[:, c0:c1].astype(BF16)
        c0 = c1


def _to_bf16(w, splits=None):
    depth, k, n = w.shape
    splits = tuple(splits or (n,))
    widths = [c1 - c0 for c0, c1 in zip((0,) + splits[:-1], splits)]
    rows = k // CAST_STEPS
    assert splits[-1] == n and k % CAST_STEPS == 0 and rows % (2 * F32_SUBLANES) == 0
    out = pl.pallas_call(
        functools.partial(_cast_body, splits=splits),
        grid=(depth, CAST_STEPS),
        in_specs=[pl.BlockSpec((None, rows, n), lambda l, i: (l, i, 0))],
        out_specs=[pl.BlockSpec((None, rows, wd), lambda l, i: (l, i, 0)) for wd in widths],
        out_shape=[jax.ShapeDtypeStruct((depth, k, wd), BF16) for wd in widths],
        compiler_params=_params(2),
        name="to_bf16",
    )(w)
    return out if len(out) > 1 else out[0]


def _ffn_body(h_ref, g_ref, wg_ref, wu_ref, wd_ref, *rest, final, tiles_per_batch):
    if final:
        fg_ref, o_ref, act_ref = rest
    elif tiles_per_batch:
        head_ref, o_ref, act_ref = rest
    else:
        o_ref, act_ref = rest
    x = h_ref[...]
    if tiles_per_batch and not final:
        n_head = head_ref.shape[0]
        first = jnp.concatenate([head_ref[...], x[:x.shape[0] - n_head]], axis=0)
        x = jnp.where(pl.program_id(0) % tiles_per_batch == 0, first, x)
    xn = _rmsnorm(x, g_ref[...]).astype(BF16)
    for c0 in range(0, D_FF, FF_CHUNK):
        c1 = min(c0 + FF_CHUNK, D_FF)
        gate = jnp.dot(xn, wg_ref[:, c0:c1], preferred_element_type=F32)
        up = jnp.dot(xn, wu_ref[:, c0:c1], preferred_element_type=F32)
        act_ref[:, c0:c1] = (gate * _sigmoid(gate) * up).astype(BF16)
    y = x + 0.5 * jnp.dot(act_ref[...], wd_ref[...], preferred_element_type=F32)
    if final:
        y = _rmsnorm(y, fg_ref[...])
    o_ref[...] = y


def _ffn(h, g, wg, wu, wd, l, final_g=None, *, tm, frames=None, head=None):
    rows = h.shape[0]
    final = final_g is not None
    row_spec = pl.BlockSpec((tm, D_MODEL), lambda i: (i, 0))
    in_spec = row_spec
    tiles_per_batch = None
    if frames is not None:
        lp, seq = frames
        if final:
            tiles_per_batch = seq // tm
            rows = rows // lp * seq
            start = lambda i: i // tiles_per_batch * lp + (lp - seq) + i % tiles_per_batch * tm
        else:
            tiles_per_batch = lp // tm
            rows = rows // seq * lp
            start = lambda i: (i // tiles_per_batch * seq
                               + jnp.maximum(i % tiles_per_batch * tm - (lp - seq), 0))
        in_spec = pl.BlockSpec((pl.Element(tm), pl.Element(D_MODEL)),
                               lambda i: (pl.multiple_of(start(i), F32_SUBLANES), 0))
    in_specs = [in_spec, _resident((1, D_MODEL)), _layer(wg, l), _layer(wu, l), _layer(wd, l)]
    args = [h, g, wg, wu, wd]
    if final:
        in_specs.append(_resident((1, D_MODEL)))
        args.append(final_g)
    elif head is not None:
        in_specs.append(_resident(head.shape))
        args.append(head)
    return pl.pallas_call(
        functools.partial(_ffn_body, final=final, tiles_per_batch=tiles_per_batch),
        grid=(rows // tm,),
        in_specs=in_specs,
        out_specs=row_spec,
        out_shape=jax.ShapeDtypeStruct((rows, D_MODEL), F32),
        scratch_shapes=[pltpu.VMEM((tm, D_FF), BF16)],
        compiler_params=_params(1),
        name="ffn_final" if final else "ffn",
    )(*args)


def _proj_body(h_ref, g_ref, w_ref, q_ref, k_ref, v_ref, u_ref, *, tm, pad):
    xn = _rmsnorm(h_ref[...], g_ref[...]).astype(BF16)

    def col(i):
        return jnp.dot(xn, w_ref[:, i * SB_WIDTH:(i + 1) * SB_WIDTH], preferred_element_type=F32)

    q_ref[...] = (col(0) * (-1.0 / math.sqrt(SB_HEAD_DIM))).astype(BF16)
    k_ref[...] = col(1).astype(BF16)
    v_ref[...] = col(2).astype(BF16)
    u = col(3) * _sigmoid(col(4))
    pos = pl.program_id(1) * tm + lax.broadcasted_iota(jnp.int32, (tm, 1), 0)
    u_ref[...] = jnp.where(pos >= pad, u, 0.0)


def _proj(h, g, w, l, *, batch, lp, pad, tm):
    rows = h.shape[0]
    nt = lp // tm
    row = lambda width: pl.BlockSpec((tm, width), lambda b, j: (b * nt + j, 0))
    bf = jax.ShapeDtypeStruct((rows, SB_WIDTH), BF16)
    return pl.pallas_call(
        functools.partial(_proj_body, tm=tm, pad=pad),
        grid=(batch, nt),
        in_specs=[row(D_MODEL), _resident((1, D_MODEL)), _layer(w, l)],
        out_specs=[row(SB_WIDTH), row(SB_WIDTH), row(SB_WIDTH), row(CONV_CH)],
        out_shape=[bf, bf, bf, jax.ShapeDtypeStruct((rows, CONV_CH), F32)],
        compiler_params=_params(2),
        name="mixer_proj",
    )(h, g, w)


def _attn_body(q_ref, k_ref, v_ref, o_ref, acc_ref, carry_ref, *, pad):
    blk = ATT_BLOCK
    i = pl.program_id(1)
    row = lax.broadcasted_iota(jnp.int32, (blk, blk), 0)
    lane = lax.broadcasted_iota(jnp.int32, (blk, blk), 1)
    qpos = i * blk + row
    half = jnp.concatenate([(row > lane).astype(BF16), jnp.ones((blk, blk), BF16)], axis=1)
    suffix_and_total = jnp.concatenate([half, half], axis=0)
    first = (lane < SB_HEAD_DIM).astype(BF16)
    second = (lane >= SB_HEAD_DIM).astype(BF16)
    nt = (((1,), (1,)), ((), ()))

    pair_cols = [slice(m * HEAD_PAIR, (m + 1) * HEAD_PAIR) for m in range(SB_HEADS // 2)]
    head_groups = [range(g, g + SB_HEADS // ATT_HEAD_GROUPS)
                   for g in range(0, SB_HEADS, SB_HEADS // ATT_HEAD_GROUPS)]

    def sweep(blocks, carry, acc):
        masked, keys, values = [], [], []
        for j, nr in blocks:
            start = pl.multiple_of(jnp.maximum(j, 0) * blk, blk)
            kpos = start + lane
            valid = (kpos < qpos) & (kpos >= pad) & (j >= 0)
            masked.append(jnp.where(valid, 0.0, MASKED)[:nr])
            keys.append(k_ref[pl.ds(start, blk), :])
            values.append(v_ref[pl.ds(start, blk), :])
        q_pair = [(q_ref[:, c] * first, q_ref[:, c] * second) for c in pair_cols]
        zneg = [[lax.dot_general(jnp.concatenate([qa[:nr], qb[:nr]], axis=0), kt[:, c], nt,
                                 preferred_element_type=F32)
                 for c, (qa, qb) in zip(pair_cols, q_pair)]
                for (_, nr), kt in zip(blocks, keys)]
        log_beta, sums = {}, {}
        for heads in head_groups:
            split = []
            for b, (_, nr) in enumerate(blocks):
                for h in heads:
                    zn = zneg[b][h // 2][(h % 2) * nr:(h % 2 + 1) * nr] + masked[b]
                    lk = jnp.minimum(zn, 0.0) - jnp.log(1.0 + jnp.exp2(jnp.abs(zn) * (-LOG2E)))
                    hi = lk.astype(BF16)
                    lo = (lk - hi.astype(F32)).astype(BF16)
                    log_beta[b, h] = lk - zn
                    split.append(jnp.concatenate([hi, lo], axis=1))
            group_sums = jnp.dot(jnp.concatenate(split, axis=0), suffix_and_total,
                                 preferred_element_type=F32)
            r0 = 0
            for b, (_, nr) in enumerate(blocks):
                for h in heads:
                    sums[b, h] = group_sums[r0:r0 + nr]
                    r0 += nr

        def add_rows(old, new):
            if old is None:
                assert new.shape[0] == blk
                return new
            nr = new.shape[0]
            top = old[:nr] + new
            return top if nr == blk else jnp.concatenate([top, old[nr:]], axis=0)

        carry, acc = list(carry), list(acc)
        states = []
        for b, (_, nr) in enumerate(blocks):
            probs = []
            for h in range(SB_HEADS):
                after = sums[b, h][:, :blk]
                total = sums[b, h][:, blk:]
                if carry[h] is not None:
                    after = after + carry[h][:nr]
                probs.append(jnp.exp(log_beta[b, h] + after).astype(BF16))
                carry[h] = add_rows(carry[h], total)
            for m, c in enumerate(pair_cols):
                both = jnp.dot(jnp.concatenate(probs[2 * m:2 * m + 2], axis=0), values[b][:, c],
                               preferred_element_type=F32)
                own = lax.broadcasted_iota(jnp.int32, (nr, HEAD_PAIR), 1) < SB_HEAD_DIM
                acc[m] = add_rows(acc[m], jnp.where(own, both[:nr], both[nr:]))
            states.append((list(carry), list(acc)))
        return states

    def store_state(carry, acc):
        for h in range(SB_HEADS):
            carry_ref[h] = carry[h]
        for m, c in enumerate(pair_cols):
            acc_ref[:, c] = acc[m]

    def is_live(carry):
        return (jnp.max(functools.reduce(jnp.maximum, carry)) > SKIP_LOG).astype(jnp.int32)

    fixed = [(i - d, blk) for d in range(ATT_FIXED - 1)] + [(i - (ATT_FIXED - 1), blk // 2)]
    states = sweep(fixed, [None] * SB_HEADS, [None] * (SB_HEADS // 2))
    store_state(*states[-2])
    carry, acc = states[-1]
    live = is_live(carry)
    o_ref[...] = jnp.concatenate(acc, axis=1).astype(BF16)

    def body(state):
        j, _ = state
        carry, acc = sweep([(j, blk)], [carry_ref[h] for h in range(SB_HEADS)],
                           [acc_ref[:, c] for c in pair_cols])[-1]
        store_state(carry, acc)
        return j - 1, is_live(carry)

    lax.while_loop(lambda s: (s[0] >= 0) & (s[1] > 0), body, (i - (ATT_FIXED - 1), live))

    @pl.when(live > 0)
    def _():
        o_ref[...] = acc_ref[...].astype(BF16)


def _attention(q, k, v, *, batch, lp, pad):
    nq = lp // ATT_BLOCK
    qspec = pl.BlockSpec((ATT_BLOCK, SB_WIDTH), lambda b, i: (b * nq + i, 0))
    kvspec = pl.BlockSpec((lp, SB_WIDTH), lambda b, i: (b, 0))
    return pl.pallas_call(
        functools.partial(_attn_body, pad=pad),
        grid=(batch, nq),
        in_specs=[qspec, kvspec, kvspec],
        out_specs=qspec,
        out_shape=jax.ShapeDtypeStruct(q.shape, BF16),
        scratch_shapes=[pltpu.VMEM((ATT_BLOCK, SB_WIDTH), F32),
                        pltpu.VMEM((SB_HEADS, ATT_BLOCK, ATT_BLOCK), F32)],
        compiler_params=_params(2),
        name="sb_attention",
    )(q, k, v)


def _mix_body(h_ref, g_ref, wgate_ref, a_ref, u_ref, uprev_ref, cw_ref, cb_ref, lng_ref, lnb_ref,
              wao_ref, wco_ref, wo_ref, o_ref, ubuf_ref, shift_ref, conv_ref, *, tm):
    ubuf_ref[0:CONV_HALO, :] = uprev_ref[...]
    ubuf_ref[CONV_HALO:, :] = u_ref[...]
    base = CONV_HALO - (CONV_K - 1)
    sub = F32_SUBLANES
    for r in range(sub):
        n = tm + sub * ((CONV_K - 1 - r) // sub)
        shift_ref[r, 0:n // sub] = ubuf_ref[base + r:base + r + n, :].reshape(n // sub, sub, CONV_CH)

    def conv_chunk(c0):
        y = jnp.broadcast_to(cb_ref[...], (CONV_CHUNK // sub, sub, CONV_CH))
        for k in range(CONV_K):
            g0 = (c0 + k - k % sub) // sub
            y = y + cw_ref[k] * shift_ref[k % sub, g0:g0 + CONV_CHUNK // sub]
        y = y.reshape(CONV_CHUNK, CONV_CH)
        mu = jnp.mean(y, axis=-1, keepdims=True)
        yc = y - mu
        var = jnp.mean(yc * yc, axis=-1, keepdims=True)
        ln = yc * lax.rsqrt(var + LN_EPS) * lng_ref[...] + lnb_ref[...]
        conv_ref[c0:c0 + CONV_CHUNK, :] = (ln * _sigmoid(ln)).astype(BF16)

    for r0 in range(0, tm, MIX_ROWS):
        rows = slice(r0, r0 + MIX_ROWS)
        for c0 in range(r0, r0 + MIX_ROWS, CONV_CHUNK):
            conv_chunk(c0)
        x = h_ref[rows, :]
        xn = _rmsnorm(x, g_ref[...]).astype(BF16)
        gate_sb = _sigmoid(jnp.dot(xn, wgate_ref[:, :D_MODEL], preferred_element_type=F32))
        gate_cv = _sigmoid(jnp.dot(xn, wgate_ref[:, D_MODEL:], preferred_element_type=F32))
        y_sb = jnp.dot(a_ref[rows, :], wao_ref[...], preferred_element_type=F32)
        y_cv = jnp.dot(conv_ref[rows, :], wco_ref[...], preferred_element_type=F32)
        m = (gate_sb * y_sb + gate_cv * y_cv).astype(BF16)
        o_ref[rows, :] = x + jnp.dot(m, wo_ref[...], preferred_element_type=F32)


def _mix(h, g, wgate, a, u, cw, cb, lng, lnb, wao, wco, wo, l, *, tm):
    rows = h.shape[0]
    halo_per_tile = tm // CONV_HALO
    row = lambda width: pl.BlockSpec((tm, width), lambda i: (i, 0))
    prev = pl.BlockSpec((CONV_HALO, CONV_CH), lambda i: (jnp.maximum(i * halo_per_tile - 1, 0), 0))
    shift_rows = tm + F32_SUBLANES * ((CONV_K - 1) // F32_SUBLANES)
    return pl.pallas_call(
        functools.partial(_mix_body, tm=tm),
        grid=(rows // tm,),
        in_specs=[row(D_MODEL), _resident((1, D_MODEL)), _layer(wgate, l), row(SB_WIDTH),
                  row(CONV_CH), prev, _resident(cw.shape), _resident((1, CONV_CH)),
                  _resident((1, CONV_CH)), _resident((1, CONV_CH)), _layer(wao, l),
                  _layer(wco, l), _layer(wo, l)],
        out_specs=row(D_MODEL),
        out_shape=jax.ShapeDtypeStruct((rows, D_MODEL), F32),
        scratch_shapes=[pltpu.VMEM((CONV_HALO + tm, CONV_CH), F32),
                        pltpu.VMEM((F32_SUBLANES, shift_rows // F32_SUBLANES, F32_SUBLANES, CONV_CH), F32),
                        pltpu.VMEM((tm, CONV_CH), BF16)],
        compiler_params=_params(1),
        name="mixer_out",
    )(h, g, wgate, a, u, u, cw, cb, lng, lnb, wao, wco, wo)


def kernel(x, meta, ffn1_norm, ffn1_w_gate, ffn1_w_up, ffn1_w_down, mix_norm, w_in, w_attn_out,
           conv_w, conv_b, conv_ln_g, conv_ln_b, w_conv_out, w_out, ffn2_norm, ffn2_w_gate,
           ffn2_w_up, ffn2_w_down, final_norm):
    batch, seq, d = x.shape
    depth = w_in.shape[0]
    assert d == D_MODEL and meta.shape == (N_META, D_MODEL) and seq % ATT_BLOCK == 0
    assert N_META <= ATT_BLOCK
    pad = ATT_BLOCK - N_META
    lp = seq + ATT_BLOCK
    tm_ffn = 1040
    tm_mix = 640
    tm_out = 1024
    assert lp % tm_ffn == 0 and lp % tm_mix == 0 and tm_mix % CONV_HALO == 0
    assert tm_mix % CONV_CHUNK == 0 and seq % tm_out == 0

    head = jnp.concatenate([jnp.zeros((pad, d), x.dtype), meta.astype(x.dtype)], axis=0)
    h = x.reshape(batch * seq, d)

    vec = lambda p: p.reshape(1, -1).astype(F32)
    conv_taps = lambda w: jnp.broadcast_to(w.astype(F32)[:, None, :], (CONV_K, F32_SUBLANES, CONV_CH))
    ffn1 = [_to_bf16(w) for w in (ffn1_w_gate, ffn1_w_up, ffn1_w_down)]
    ffn2 = [_to_bf16(w) for w in (ffn2_w_gate, ffn2_w_up, ffn2_w_down)]
    w_proj, w_gates = _to_bf16(w_in, (3 * SB_WIDTH + 2 * CONV_CH, w_in.shape[2]))
    w_ao, w_co, w_o = _to_bf16(w_attn_out), _to_bf16(w_conv_out), _to_bf16(w_out)
    for l in range(depth):
        ends = dict(frames=(lp, seq), head=head) if l == 0 else {}
        h = _ffn(h, vec(ffn1_norm[l]), *ffn1, l, tm=tm_ffn, **ends)
        q, k, v, u = _proj(h, vec(mix_norm[l]), w_proj, l, batch=batch, lp=lp, pad=pad, tm=tm_mix)
        a = _attention(q, k, v, batch=batch, lp=lp, pad=pad)
        h = _mix(h, vec(mix_norm[l]), w_gates, a, u, conv_taps(conv_w[l]), vec(conv_b[l]),
                 vec(conv_ln_g[l]), vec(conv_ln_b[l]), w_ao, w_co, w_o, l, tm=tm_mix)
        if l < depth - 1:
            h = _ffn(h, vec(ffn2_norm[l]), *ffn2, l, tm=tm_ffn)
    out = _ffn(h, vec(ffn2_norm[depth - 1]), *ffn2, depth - 1, vec(final_norm), tm=tm_out,
               frames=(lp, seq))
    return out.reshape(batch, seq, d)
```

```python
import functools
import math

import jax
import jax.numpy as jnp
from jax import lax
from jax.experimental import pallas as pl
from jax.experimental.pallas import tpu as pltpu

D_MODEL = 1024
N_META = 16
SB_HEADS = 8
SB_HEAD_DIM = 64
SB_WIDTH = SB_HEADS * SB_HEAD_DIM
HEAD_PAIR = 2 * SB_HEAD_DIM
CONV_CH = 512
CONV_K = 31
D_FF = 2816
NORM_EPS = 1e-6
LN_EPS = 1e-5
LOG2E = 1.4426950408889634

F32_SUBLANES = 8
ATT_BLOCK = 128
ATT_HEAD_GROUPS = 2
ATT_QBLOCKS = 5
ATT_FIXED = 3
MXU_COLS = 256
FF_CHUNK = MXU_COLS
CONV_HALO = 32
CONV_CHUNK = 32
MIX_ROWS = 320
CAST_STEPS = 2
VMEM_LIMIT = 56 * 1024 * 1024

SKIP_LOG = -104.0
MASKED = 1e30

F32 = jnp.float32
BF16 = jnp.bfloat16


def _rmsnorm(x, g):
    return x * lax.rsqrt(jnp.mean(x * x, axis=-1, keepdims=True) + NORM_EPS) * g


def _sigmoid(x):
    return 1.0 / (1.0 + jnp.exp(-x))


def _resident(shape):
    return pl.BlockSpec(shape, lambda *_: (0,) * len(shape), pipeline_mode=pl.Buffered(1))


def _layer(w, l):
    return pl.BlockSpec((None,) + w.shape[1:], lambda *_: (l, 0, 0), pipeline_mode=pl.Buffered(1))


def _params(n_axes):
    return pltpu.CompilerParams(dimension_semantics=("arbitrary",) * n_axes,
                                vmem_limit_bytes=VMEM_LIMIT)


def _cast_body(w_ref, *o_refs, splits):
    c0 = 0
    for o_ref, c1 in zip(o_refs, splits):
        o_ref[...] = w_ref[:, c0:c1].astype(BF16)
        c0 = c1


def _to_bf16(w, splits=None):
    depth, k, n = w.shape
    splits = tuple(splits or (n,))
    widths = [c1 - c0 for c0, c1 in zip((0,) + splits[:-1], splits)]
    rows = k // CAST_STEPS
    assert splits[-1] == n and k % CAST_STEPS == 0 and rows % (2 * F32_SUBLANES) == 0
    out = pl.pallas_call(
        functools.partial(_cast_body, splits=splits),
        grid=(depth, CAST_STEPS),
        in_specs=[pl.BlockSpec((None, rows, n), lambda l, i: (l, i, 0))],
        out_specs=[pl.BlockSpec((None, rows, wd), lambda l, i: (l, i, 0)) for wd in widths],
        out_shape=[jax.ShapeDtypeStruct((depth, k, wd), BF16) for wd in widths],
        compiler_params=_params(2),
        name="to_bf16",
    )(w)
    return out if len(out) > 1 else out[0]


def _ffn_body(h_ref, g_ref, wg_ref, wu_ref, wd_ref, *rest, final, tiles_per_batch):
    if final:
        fg_ref, o_ref, act_ref = rest
    elif tiles_per_batch:
        head_ref, o_ref, act_ref = rest
    else:
        o_ref, act_ref = rest
    x = h_ref[...]
    if tiles_per_batch and not final:
        n_head = head_ref.shape[0]
        first = jnp.concatenate([head_ref[...], x[:x.shape[0] - n_head]], axis=0)
        x = jnp.where(pl.program_id(0) % tiles_per_batch == 0, first, x)
    xn = _rmsnorm(x, g_ref[...]).astype(BF16)
    for c0 in range(0, D_FF, FF_CHUNK):
        c1 = min(c0 + FF_CHUNK, D_FF)
        gate = jnp.dot(xn, wg_ref[:, c0:c1], preferred_element_type=F32)
        up = jnp.dot(xn, wu_ref[:, c0:c1], preferred_element_type=F32)
        act_ref[:, c0:c1] = (gate * _sigmoid(gate) * up).astype(BF16)
    y = x + 0.5 * jnp.dot(act_ref[...], wd_ref[...], preferred_element_type=F32)
    if final:
        y = _rmsnorm(y, fg_ref[...])
    o_ref[...] = y


def _ffn(h, g, wg, wu, wd, l, final_g=None, *, tm, frames=None, head=None):
    rows = h.shape[0]
    final = final_g is not None
    row_spec = pl.BlockSpec((tm, D_MODEL), lambda i: (i, 0))
    in_spec = row_spec
    tiles_per_batch = None
    if frames is not None:
        lp, seq = frames
        if final:
            tiles_per_batch = seq // tm
            rows = rows // lp * seq
            start = lambda i: i // tiles_per_batch * lp + (lp - seq) + i % tiles_per_batch * tm
        else:
            tiles_per_batch = lp // tm
            rows = rows // seq * lp
            start = lambda i: (i // tiles_per_batch * seq
                               + jnp.maximum(i % tiles_per_batch * tm - (lp - seq), 0))
        in_spec = pl.BlockSpec((pl.Element(tm), pl.Element(D_MODEL)),
                               lambda i: (pl.multiple_of(start(i), F32_SUBLANES), 0))
    in_specs = [in_spec, _resident((1, D_MODEL)), _layer(wg, l), _layer(wu, l), _layer(wd, l)]
    args = [h, g, wg, wu, wd]
    if final:
        in_specs.append(_resident((1, D_MODEL)))
        args.append(final_g)
    elif head is not None:
        in_specs.append(_resident(head.shape))
        args.append(head)
    return pl.pallas_call(
        functools.partial(_ffn_body, final=final, tiles_per_batch=tiles_per_batch),
        grid=(rows // tm,),
        in_specs=in_specs,
        out_specs=row_spec,
        out_shape=jax.ShapeDtypeStruct((rows, D_MODEL), F32),
        scratch_shapes=[pltpu.VMEM((tm, D_FF), BF16)],
        compiler_params=_params(1),
        name="ffn_final" if final else "ffn",
    )(*args)


def _proj_body(h_ref, g_ref, w_ref, q_ref, k_ref, v_ref, u_ref, *, tm, pad):
    xn = _rmsnorm(h_ref[...], g_ref[...]).astype(BF16)

    def col(i):
        return jnp.dot(xn, w_ref[:, i * SB_WIDTH:(i + 1) * SB_WIDTH], preferred_element_type=F32)

    q_ref[...] = (col(0) * (-1.0 / math.sqrt(SB_HEAD_DIM))).astype(BF16)
    k_ref[...] = col(1).astype(BF16)
    v_ref[...] = col(2).astype(BF16)
    u = col(3) * _sigmoid(col(4))
    pos = pl.program_id(1) * tm + lax.broadcasted_iota(jnp.int32, (tm, 1), 0)
    u_ref[...] = jnp.where(pos >= pad, u, 0.0)


def _proj(h, g, w, l, *, batch, lp, pad, tm):
    rows = h.shape[0]
    nt = lp // tm
    row = lambda width: pl.BlockSpec((tm, width), lambda b, j: (b * nt + j, 0))
    bf = jax.ShapeDtypeStruct((rows, SB_WIDTH), BF16)
    return pl.pallas_call(
        functools.partial(_proj_body, tm=tm, pad=pad),
        grid=(batch, nt),
        in_specs=[row(D_MODEL), _resident((1, D_MODEL)), _layer(w, l)],
        out_specs=[row(SB_WIDTH), row(SB_WIDTH), row(SB_WIDTH), row(CONV_CH)],
        out_shape=[bf, bf, bf, jax.ShapeDtypeStruct((rows, CONV_CH), F32)],
        compiler_params=_params(2),
        name="mixer_proj",
    )(h, g, w)


def _attn_body(q_ref, k_ref, v_ref, o_ref, acc_ref, carry_ref, *, pad):
    blk = ATT_BLOCK
    i0 = pl.program_id(1) * ATT_QBLOCKS
    row = lax.broadcasted_iota(jnp.int32, (blk, blk), 0)
    lane = lax.broadcasted_iota(jnp.int32, (blk, blk), 1)
    half = jnp.concatenate([(row > lane).astype(BF16), jnp.ones((blk, blk), BF16)], axis=1)
    suffix_and_total = jnp.concatenate([half, half], axis=0)
    first = (lane < SB_HEAD_DIM).astype(BF16)
    second = (lane >= SB_HEAD_DIM).astype(BF16)
    nt = (((1,), (1,)), ((), ()))

    pair_cols = [slice(m * HEAD_PAIR, (m + 1) * HEAD_PAIR) for m in range(SB_HEADS // 2)]
    head_groups = [range(g, g + SB_HEADS // ATT_HEAD_GROUPS)
                   for g in range(0, SB_HEADS, SB_HEADS // ATT_HEAD_GROUPS)]
    block_rows = lambda t: slice(t * blk, (t + 1) * blk)

    def add_rows(old, new):
        if old is None:
            assert new.shape[0] == blk
            return new
        nr = new.shape[0]
        top = old[:nr] + new
        return top if nr == blk else jnp.concatenate([top, old[nr:]], axis=0)

    def sweep(units, carry, acc):
        keys, masked = {}, []
        for t, key, j, nr in units:
            if key not in keys:
                start = pl.multiple_of(jnp.maximum(j, 0) * blk, blk)
                keys[key] = (start, j, k_ref[pl.ds(start, blk), :], v_ref[pl.ds(start, blk), :])
            start, jj = keys[key][:2]
            valid = (start + lane < (i0 + t) * blk + row) & (start + lane >= pad) & (jj >= 0)
            masked.append(jnp.where(valid, 0.0, MASKED)[:nr])
        visitors = {key: [u for u, unit in enumerate(units) if unit[1] == key] for key in keys}
        q_pair = {t: [(q_ref[block_rows(t), c] * first, q_ref[block_rows(t), c] * second)
                      for c in pair_cols] for t in {unit[0] for unit in units}}

        def per_visitor(key, m, stacked):
            out, r0 = {}, 0
            for u in visitors[key]:
                nr = units[u][3]
                out[u, m] = stacked[r0:r0 + 2 * nr]
                r0 += 2 * nr
            return out

        zneg = {}
        for key, (_, _, kt, _) in keys.items():
            for m, c in enumerate(pair_cols):
                lhs = [q[:units[u][3]] for u in visitors[key] for q in q_pair[units[u][0]][m]]
                zneg.update(per_visitor(key, m, lax.dot_general(
                    jnp.concatenate(lhs, axis=0), kt[:, c], nt, preferred_element_type=F32)))
        log_beta, sums = {}, {}
        for heads in head_groups:
            split = []
            for u, (_, _, _, nr) in enumerate(units):
                for h in heads:
                    zn = zneg[u, h // 2][(h % 2) * nr:(h % 2 + 1) * nr] + masked[u]
                    lk = jnp.minimum(zn, 0.0) - jnp.log(1.0 + jnp.exp2(jnp.abs(zn) * (-LOG2E)))
                    hi = lk.astype(BF16)
                    lo = (lk - hi.astype(F32)).astype(BF16)
                    log_beta[u, h] = lk - zn
                    split.append(jnp.concatenate([hi, lo], axis=1))
            group_sums = jnp.dot(jnp.concatenate(split, axis=0), suffix_and_total,
                                 preferred_element_type=F32)
            r0 = 0
            for u, (_, _, _, nr) in enumerate(units):
                for h in heads:
                    sums[u, h] = group_sums[r0:r0 + nr]
                    r0 += nr

        carry = {t: list(c) for t, c in carry.items()}
        acc = {t: list(a) for t, a in acc.items()}
        probs, carry_after = {}, []
        for u, (t, _, _, nr) in enumerate(units):
            for h in range(SB_HEADS):
                after = sums[u, h][:, :blk]
                if carry[t][h] is not None:
                    after = after + carry[t][h][:nr]
                probs[u, h] = jnp.exp(log_beta[u, h] + after).astype(BF16)
                carry[t][h] = add_rows(carry[t][h], sums[u, h][:, blk:])
            carry_after.append(list(carry[t]))
        outs = {}
        for key, (_, _, _, vt) in keys.items():
            for m, c in enumerate(pair_cols):
                lhs = [probs[u, 2 * m + s] for u in visitors[key] for s in range(2)]
                outs.update(per_visitor(key, m, jnp.dot(jnp.concatenate(lhs, axis=0), vt[:, c],
                                                        preferred_element_type=F32)))
        states = []
        for u, (t, _, _, nr) in enumerate(units):
            own = lax.broadcasted_iota(jnp.int32, (nr, HEAD_PAIR), 1) < SB_HEAD_DIM
            for m in range(SB_HEADS // 2):
                acc[t][m] = add_rows(acc[t][m], jnp.where(own, outs[u, m][:nr], outs[u, m][nr:]))
            states.append((carry_after[u], list(acc[t])))
        return states

    def store_state(t, carry, acc):
        for h in range(SB_HEADS):
            carry_ref[t, h] = carry[h]
        for m, c in enumerate(pair_cols):
            acc_ref[block_rows(t), c] = acc[m]

    def is_live(carry):
        return (jnp.max(functools.reduce(jnp.maximum, carry)) > SKIP_LOG).astype(jnp.int32)

    query_blocks = range(ATT_QBLOCKS)
    fixed = [(t, t - d, i0 + t - d, blk if d < ATT_FIXED - 1 else blk // 2)
             for t in query_blocks for d in range(ATT_FIXED)]
    states = sweep(fixed, {t: [None] * SB_HEADS for t in query_blocks},
                   {t: [None] * (SB_HEADS // 2) for t in query_blocks})
    live = []
    for t in query_blocks:
        store_state(t, *states[ATT_FIXED * t + ATT_FIXED - 2])
        carry, acc = states[ATT_FIXED * t + ATT_FIXED - 1]
        live.append(is_live(carry))
        o_ref[block_rows(t), :] = jnp.concatenate(acc, axis=1).astype(BF16)

    for t in query_blocks:
        def body(state, t=t):
            j, _ = state
            carry, acc = sweep([(t, 0, j, blk)], {t: [carry_ref[t, h] for h in range(SB_HEADS)]},
                               {t: [acc_ref[block_rows(t), c] for c in pair_cols]})[-1]
            store_state(t, carry, acc)
            return j - 1, is_live(carry)

        lax.while_loop(lambda s: (s[0] >= 0) & (s[1] > 0), body,
                       (i0 + t - (ATT_FIXED - 1), live[t]))

        @pl.when(live[t] > 0)
        def _(t=t):
            o_ref[block_rows(t), :] = acc_ref[block_rows(t), :].astype(BF16)


def _attention(q, k, v, *, batch, lp, pad):
    steps = lp // (ATT_BLOCK * ATT_QBLOCKS)
    assert lp == steps * ATT_BLOCK * ATT_QBLOCKS
    qspec = pl.BlockSpec((ATT_QBLOCKS * ATT_BLOCK, SB_WIDTH), lambda b, g: (b * steps + g, 0))
    kvspec = pl.BlockSpec((lp, SB_WIDTH), lambda b, g: (b, 0))
    return pl.pallas_call(
        functools.partial(_attn_body, pad=pad),
        grid=(batch, steps),
        in_specs=[qspec, kvspec, kvspec],
        out_specs=qspec,
        out_shape=jax.ShapeDtypeStruct(q.shape, BF16),
        scratch_shapes=[pltpu.VMEM((ATT_QBLOCKS * ATT_BLOCK, SB_WIDTH), F32),
                        pltpu.VMEM((ATT_QBLOCKS, SB_HEADS, ATT_BLOCK, ATT_BLOCK), F32)],
        compiler_params=_params(2),
        name="sb_attention",
    )(q, k, v)


def _mix_body(h_ref, g_ref, wgate_ref, a_ref, u_ref, uprev_ref, cw_ref, cb_ref, lng_ref, lnb_ref,
              wao_ref, wco_ref, wo_ref, o_ref, ubuf_ref, shift_ref, conv_ref, *, tm):
    ubuf_ref[0:CONV_HALO, :] = uprev_ref[...]
    ubuf_ref[CONV_HALO:, :] = u_ref[...]
    base = CONV_HALO - (CONV_K - 1)
    sub = F32_SUBLANES
    for r in range(sub):
        n = tm + sub * ((CONV_K - 1 - r) // sub)
        shift_ref[r, 0:n // sub] = ubuf_ref[base + r:base + r + n, :].reshape(n // sub, sub, CONV_CH)

    def conv_chunk(c0):
        y = jnp.broadcast_to(cb_ref[...], (CONV_CHUNK // sub, sub, CONV_CH))
        for k in range(CONV_K):
            g0 = (c0 + k - k % sub) // sub
            y = y + cw_ref[k] * shift_ref[k % sub, g0:g0 + CONV_CHUNK // sub]
        y = y.reshape(CONV_CHUNK, CONV_CH)
        mu = jnp.mean(y, axis=-1, keepdims=True)
        yc = y - mu
        var = jnp.mean(yc * yc, axis=-1, keepdims=True)
        ln = yc * lax.rsqrt(var + LN_EPS) * lng_ref[...] + lnb_ref[...]
        conv_ref[c0:c0 + CONV_CHUNK, :] = (ln * _sigmoid(ln)).astype(BF16)

    for r0 in range(0, tm, MIX_ROWS):
        rows = slice(r0, r0 + MIX_ROWS)
        for c0 in range(r0, r0 + MIX_ROWS, CONV_CHUNK):
            conv_chunk(c0)
        x = h_ref[rows, :]
        xn = _rmsnorm(x, g_ref[...]).astype(BF16)
        gate_sb = _sigmoid(jnp.dot(xn, wgate_ref[:, :D_MODEL], preferred_element_type=F32))
        gate_cv = _sigmoid(jnp.dot(xn, wgate_ref[:, D_MODEL:], preferred_element_type=F32))
        y_sb = jnp.dot(a_ref[rows, :], wao_ref[...], preferred_element_type=F32)
        y_cv = jnp.dot(conv_ref[rows, :], wco_ref[...], preferred_element_type=F32)
        m = (gate_sb * y_sb + gate_cv * y_cv).astype(BF16)
        o_ref[rows, :] = x + jnp.dot(m, wo_ref[...], preferred_element_type=F32)


def _mix(h, g, wgate, a, u, cw, cb, lng, lnb, wao, wco, wo, l, *, tm):
    rows = h.shape[0]
    halo_per_tile = tm // CONV_HALO
    row = lambda width: pl.BlockSpec((tm, width), lambda i: (i, 0))
    prev = pl.BlockSpec((CONV_HALO, CONV_CH), lambda i: (jnp.maximum(i * halo_per_tile - 1, 0), 0))
    shift_rows = tm + F32_SUBLANES * ((CONV_K - 1) // F32_SUBLANES)
    return pl.pallas_call(
        functools.partial(_mix_body, tm=tm),
        grid=(rows // tm,),
        in_specs=[row(D_MODEL), _resident((1, D_MODEL)), _layer(wgate, l), row(SB_WIDTH),
                  row(CONV_CH), prev, _resident(cw.shape), _resident((1, CONV_CH)),
                  _resident((1, CONV_CH)), _resident((1, CONV_CH)), _layer(wao, l),
                  _layer(wco, l), _layer(wo, l)],
        out_specs=row(D_MODEL),
        out_shape=jax.ShapeDtypeStruct((rows, D_MODEL), F32),
        scratch_shapes=[pltpu.VMEM((CONV_HALO + tm, CONV_CH), F32),
                        pltpu.VMEM((F32_SUBLANES, shift_rows // F32_SUBLANES, F32_SUBLANES, CONV_CH), F32),
                        pltpu.VMEM((tm, CONV_CH), BF16)],
        compiler_params=_params(1),
        name="mixer_out",
    )(h, g, wgate, a, u, u, cw, cb, lng, lnb, wao, wco, wo)


def kernel(x, meta, ffn1_norm, ffn1_w_gate, ffn1_w_up, ffn1_w_down, mix_norm, w_in, w_attn_out,
           conv_w, conv_b, conv_ln_g, conv_ln_b, w_conv_out, w_out, ffn2_norm, ffn2_w_gate,
           ffn2_w_up, ffn2_w_down, final_norm):
    batch, seq, d = x.shape
    depth = w_in.shape[0]
    assert d == D_MODEL and meta.shape == (N_META, D_MODEL) and seq % ATT_BLOCK == 0
    assert N_META <= ATT_BLOCK
    pad = ATT_BLOCK - N_META
    lp = seq + ATT_BLOCK
    tm_ffn = 1040
    tm_mix = 640
    tm_out = 1024
    assert lp % tm_ffn == 0 and lp % tm_mix == 0 and tm_mix % CONV_HALO == 0
    assert tm_mix % CONV_CHUNK == 0 and seq % tm_out == 0

    head = jnp.concatenate([jnp.zeros((pad, d), x.dtype), meta.astype(x.dtype)], axis=0)
    h = x.reshape(batch * seq, d)

    vec = lambda p: p.reshape(1, -1).astype(F32)
    conv_taps = lambda w: jnp.broadcast_to(w.astype(F32)[:, None, :], (CONV_K, F32_SUBLANES, CONV_CH))
    ffn1 = [_to_bf16(w) for w in (ffn1_w_gate, ffn1_w_up, ffn1_w_down)]
    ffn2 = [_to_bf16(w) for w in (ffn2_w_gate, ffn2_w_up, ffn2_w_down)]
    w_proj, w_gates = _to_bf16(w_in, (3 * SB_WIDTH + 2 * CONV_CH, w_in.shape[2]))
    w_ao, w_co, w_o = _to_bf16(w_attn_out), _to_bf16(w_conv_out), _to_bf16(w_out)
    for l in range(depth):
        ends = dict(frames=(lp, seq), head=head) if l == 0 else {}
        h = _ffn(h, vec(ffn1_norm[l]), *ffn1, l, tm=tm_ffn, **ends)
        q, k, v, u = _proj(h, vec(mix_norm[l]), w_proj, l, batch=batch, lp=lp, pad=pad, tm=tm_ffn)
        a = _attention(q, k, v, batch=batch, lp=lp, pad=pad)
        h = _mix(h, vec(mix_norm[l]), w_gates, a, u, conv_taps(conv_w[l]), vec(conv_b[l]),
                 vec(conv_ln_g[l]), vec(conv_ln_b[l]), w_ao, w_co, w_o, l, tm=tm_mix)
        if l < depth - 1:
            h = _ffn(h, vec(ffn2_norm[l]), *ffn2, l, tm=tm_ffn)
    out = _ffn(h, vec(ffn2_norm[depth - 1]), *ffn2, depth - 1, vec(final_norm), tm=tm_out,
               frames=(lp, seq))
    return out.reshape(batch, seq, d)
```

```python
import functools
import math

import jax
import jax.numpy as jnp
from jax import lax
from jax.experimental import pallas as pl
from jax.experimental.pallas import tpu as pltpu

D_MODEL = 1024
N_META = 16
SB_HEADS = 8
SB_HEAD_DIM = 64
SB_WIDTH = SB_HEADS * SB_HEAD_DIM
HEAD_PAIR = 2 * SB_HEAD_DIM
CONV_CH = 512
CONV_K = 31
D_FF = 2816
NORM_EPS = 1e-6
LN_EPS = 1e-5
LOG2E = 1.4426950408889634

F32_SUBLANES = 8
ATT_BLOCK = 128
ATT_HEAD_GROUPS = 2
ATT_QBLOCKS = 5
ATT_FIXED = 3
MXU_COLS = 256
FF_CHUNK = MXU_COLS
CONV_HALO = 32
CONV_CHUNK = 32
MIX_ROWS = 320
CAST_STEPS = 2
VMEM_LIMIT = 56 * 1024 * 1024

SKIP_LOG = -104.0
MASKED = 1e30

F32 = jnp.float32
BF16 = jnp.bfloat16


def _rmsnorm(x, g):
    return x * lax.rsqrt(jnp.mean(x * x, axis=-1, keepdims=True) + NORM_EPS) * g


def _sigmoid(x):
    return 1.0 / (1.0 + jnp.exp(-x))


def _resident(shape):
    return pl.BlockSpec(shape, lambda *_: (0,) * len(shape), pipeline_mode=pl.Buffered(1))


def _layer(w, l):
    return pl.BlockSpec((None,) + w.shape[1:], lambda *_: (l, 0, 0), pipeline_mode=pl.Buffered(1))


def _params(n_axes):
    return pltpu.CompilerParams(dimension_semantics=("arbitrary",) * n_axes,
                                vmem_limit_bytes=VMEM_LIMIT)


def _cast_body(w_ref, *o_refs, splits):
    c0 = 0
    for o_ref, c1 in zip(o_refs, splits):
        o_ref[...] = w_ref[:, c0:c1].astype(BF16)
        c0 = c1


def _to_bf16(w, splits=None):
    depth, k, n = w.shape
    splits = tuple(splits or (n,))
    widths = [c1 - c0 for c0, c1 in zip((0,) + splits[:-1], splits)]
    rows = k // CAST_STEPS
    assert splits[-1] == n and k % CAST_STEPS == 0 and rows % (2 * F32_SUBLANES) == 0
    out = pl.pallas_call(
        functools.partial(_cast_body, splits=splits),
        grid=(depth, CAST_STEPS),
        in_specs=[pl.BlockSpec((None, rows, n), lambda l, i: (l, i, 0))],
        out_specs=[pl.BlockSpec((None, rows, wd), lambda l, i: (l, i, 0)) for wd in widths],
        out_shape=[jax.ShapeDtypeStruct((depth, k, wd), BF16) for wd in widths],
        compiler_params=_params(2),
        name="to_bf16",
    )(w)
    return out if len(out) > 1 else out[0]


def _ffn_body(h_ref, g_ref, wg_ref, wu_ref, wd_ref, *rest, final, tiles_per_batch):
    if final:
        fg_ref, o_ref, act_ref = rest
    elif tiles_per_batch:
        head_ref, o_ref, act_ref = rest
    else:
        o_ref, act_ref = rest
    x = h_ref[...]
    if tiles_per_batch and not final:
        n_head = head_ref.shape[0]
        first = jnp.concatenate([head_ref[...], x[:x.shape[0] - n_head]], axis=0)
        x = jnp.where(pl.program_id(0) % tiles_per_batch == 0, first, x)
    xn = _rmsnorm(x, g_ref[...]).astype(BF16)
    for c0 in range(0, D_FF, FF_CHUNK):
        c1 = min(c0 + FF_CHUNK, D_FF)
        gate = jnp.dot(xn, wg_ref[:, c0:c1], preferred_element_type=F32)
        up = jnp.dot(xn, wu_ref[:, c0:c1], preferred_element_type=F32)
        act_ref[:, c0:c1] = (gate * _sigmoid(gate) * up).astype(BF16)
    y = x + 0.5 * jnp.dot(act_ref[...], wd_ref[...], preferred_element_type=F32)
    if final:
        y = _rmsnorm(y, fg_ref[...])
    o_ref[...] = y


def _ffn(h, g, wg, wu, wd, l, final_g=None, *, tm, frames=None, head=None):
    rows = h.shape[0]
    final = final_g is not None
    row_spec = pl.BlockSpec((tm, D_MODEL), lambda i: (i, 0))
    in_spec = row_spec
    tiles_per_batch = None
    if frames is not None:
        lp, seq = frames
        if final:
            tiles_per_batch = seq // tm
            rows = rows // lp * seq
            start = lambda i: i // tiles_per_batch * lp + (lp - seq) + i % tiles_per_batch * tm
        else:
            tiles_per_batch = lp // tm
            rows = rows // seq * lp
            start = lambda i: (i // tiles_per_batch * seq
                               + jnp.maximum(i % tiles_per_batch * tm - (lp - seq), 0))
        in_spec = pl.BlockSpec((pl.Element(tm), pl.Element(D_MODEL)),
                               lambda i: (pl.multiple_of(start(i), F32_SUBLANES), 0))
    in_specs = [in_spec, _resident((1, D_MODEL)), _layer(wg, l), _layer(wu, l), _layer(wd, l)]
    args = [h, g, wg, wu, wd]
    if final:
        in_specs.append(_resident((1, D_MODEL)))
        args.append(final_g)
    elif head is not None:
        in_specs.append(_resident(head.shape))
        args.append(head)
    return pl.pallas_call(
        functools.partial(_ffn_body, final=final, tiles_per_batch=tiles_per_batch),
        grid=(rows // tm,),
        in_specs=in_specs,
        out_specs=row_spec,
        out_shape=jax.ShapeDtypeStruct((rows, D_MODEL), F32),
        scratch_shapes=[pltpu.VMEM((tm, D_FF), BF16)],
        compiler_params=_params(1),
        name="ffn_final" if final else "ffn",
    )(*args)


def _proj_body(h_ref, g_ref, w_ref, xn_ref, q_ref, k_ref, v_ref, u_ref, *, tm, pad):
    xn = _rmsnorm(h_ref[...], g_ref[...]).astype(BF16)
    xn_ref[...] = xn

    def col(i):
        return jnp.dot(xn, w_ref[:, i * SB_WIDTH:(i + 1) * SB_WIDTH], preferred_element_type=F32)

    q_ref[...] = (col(0) * (-1.0 / math.sqrt(SB_HEAD_DIM))).astype(BF16)
    k_ref[...] = col(1).astype(BF16)
    v_ref[...] = col(2).astype(BF16)
    u = col(3) * _sigmoid(col(4))
    pos = pl.program_id(1) * tm + lax.broadcasted_iota(jnp.int32, (tm, 1), 0)
    u_ref[...] = jnp.where(pos >= pad, u, 0.0)


def _proj(h, g, w, l, *, batch, lp, pad, tm):
    rows = h.shape[0]
    nt = lp // tm
    row = lambda width: pl.BlockSpec((tm, width), lambda b, j: (b * nt + j, 0))
    bf = jax.ShapeDtypeStruct((rows, SB_WIDTH), BF16)
    return pl.pallas_call(
        functools.partial(_proj_body, tm=tm, pad=pad),
        grid=(batch, nt),
        in_specs=[row(D_MODEL), _resident((1, D_MODEL)), _layer(w, l)],
        out_specs=[row(D_MODEL), row(SB_WIDTH), row(SB_WIDTH), row(SB_WIDTH), row(CONV_CH)],
        out_shape=[jax.ShapeDtypeStruct((rows, D_MODEL), BF16), bf, bf, bf,
                   jax.ShapeDtypeStruct((rows, CONV_CH), F32)],
        compiler_params=_params(2),
        name="mixer_proj",
    )(h, g, w)


def _attn_body(q_ref, k_ref, v_ref, xn_ref, wgate_ref, o_ref, gates_ref, acc_ref, carry_ref, *, pad):
    blk = ATT_BLOCK
    gate_blocks = iter(range(0, wgate_ref.shape[1], MXU_COLS))

    def gate_matmuls(n):
        for c in [c for _, c in zip(range(n), gate_blocks)]:
            gates_ref[:, c:c + MXU_COLS] = jnp.dot(xn_ref[...], wgate_ref[:, c:c + MXU_COLS],
                                                   preferred_element_type=F32)

    i0 = pl.program_id(1) * ATT_QBLOCKS
    row = lax.broadcasted_iota(jnp.int32, (blk, blk), 0)
    lane = lax.broadcasted_iota(jnp.int32, (blk, blk), 1)
    half = jnp.concatenate([(row > lane).astype(BF16), jnp.ones((blk, blk), BF16)], axis=1)
    suffix_and_total = jnp.concatenate([half, half], axis=0)
    first = (lane < SB_HEAD_DIM).astype(BF16)
    second = (lane >= SB_HEAD_DIM).astype(BF16)
    nt = (((1,), (1,)), ((), ()))

    pair_cols = [slice(m * HEAD_PAIR, (m + 1) * HEAD_PAIR) for m in range(SB_HEADS // 2)]
    head_groups = [range(g, g + SB_HEADS // ATT_HEAD_GROUPS)
                   for g in range(0, SB_HEADS, SB_HEADS // ATT_HEAD_GROUPS)]
    block_rows = lambda t: slice(t * blk, (t + 1) * blk)

    def add_rows(old, new):
        if old is None:
            assert new.shape[0] == blk
            return new
        nr = new.shape[0]
        top = old[:nr] + new
        return top if nr == blk else jnp.concatenate([top, old[nr:]], axis=0)

    def sweep(units, carry, acc, fill=lambda n: None):
        keys, masked = {}, []
        for t, key, j, nr in units:
            if key not in keys:
                start = pl.multiple_of(jnp.maximum(j, 0) * blk, blk)
                keys[key] = (start, j, k_ref[pl.ds(start, blk), :], v_ref[pl.ds(start, blk), :])
            start, jj = keys[key][:2]
            valid = (start + lane < (i0 + t) * blk + row) & (start + lane >= pad) & (jj >= 0)
            masked.append(jnp.where(valid, 0.0, MASKED)[:nr])
        visitors = {key: [u for u, unit in enumerate(units) if unit[1] == key] for key in keys}
        q_pair = {t: [(q_ref[block_rows(t), c] * first, q_ref[block_rows(t), c] * second)
                      for c in pair_cols] for t in {unit[0] for unit in units}}

        def per_visitor(key, m, stacked):
            out, r0 = {}, 0
            for u in visitors[key]:
                nr = units[u][3]
                out[u, m] = stacked[r0:r0 + 2 * nr]
                r0 += 2 * nr
            return out

        zneg = {}
        for key, (_, _, kt, _) in keys.items():
            for m, c in enumerate(pair_cols):
                lhs = [q[:units[u][3]] for u in visitors[key] for q in q_pair[units[u][0]][m]]
                zneg.update(per_visitor(key, m, lax.dot_general(
                    jnp.concatenate(lhs, axis=0), kt[:, c], nt, preferred_element_type=F32)))
        fill(4)
        log_beta, sums = {}, {}
        for heads in head_groups:
            split = []
            for u, (_, _, _, nr) in enumerate(units):
                for h in heads:
                    zn = zneg[u, h // 2][(h % 2) * nr:(h % 2 + 1) * nr] + masked[u]
                    lk = jnp.minimum(zn, 0.0) - jnp.log(1.0 + jnp.exp2(jnp.abs(zn) * (-LOG2E)))
                    hi = lk.astype(BF16)
                    lo = (lk - hi.astype(F32)).astype(BF16)
                    log_beta[u, h] = lk - zn
                    split.append(jnp.concatenate([hi, lo], axis=1))
            group_sums = jnp.dot(jnp.concatenate(split, axis=0), suffix_and_total,
                                 preferred_element_type=F32)
            r0 = 0
            for u, (_, _, _, nr) in enumerate(units):
                for h in heads:
                    sums[u, h] = group_sums[r0:r0 + nr]
                    r0 += nr
            fill(2)

        carry = {t: list(c) for t, c in carry.items()}
        acc = {t: list(a) for t, a in acc.items()}
        probs, carry_after = {}, []
        for u, (t, _, _, nr) in enumerate(units):
            for h in range(SB_HEADS):
                after = sums[u, h][:, :blk]
                if carry[t][h] is not None:
                    after = after + carry[t][h][:nr]
                probs[u, h] = jnp.exp(log_beta[u, h] + after).astype(BF16)
                carry[t][h] = add_rows(carry[t][h], sums[u, h][:, blk:])
            carry_after.append(list(carry[t]))
        outs = {}
        for key, (_, _, _, vt) in keys.items():
            for m, c in enumerate(pair_cols):
                lhs = [probs[u, 2 * m + s] for u in visitors[key] for s in range(2)]
                outs.update(per_visitor(key, m, jnp.dot(jnp.concatenate(lhs, axis=0), vt[:, c],
                                                        preferred_element_type=F32)))
        states = []
        for u, (t, _, _, nr) in enumerate(units):
            own = lax.broadcasted_iota(jnp.int32, (nr, HEAD_PAIR), 1) < SB_HEAD_DIM
            for m in range(SB_HEADS // 2):
                acc[t][m] = add_rows(acc[t][m], jnp.where(own, outs[u, m][:nr], outs[u, m][nr:]))
            states.append((carry_after[u], list(acc[t])))
        return states

    def store_state(t, carry, acc):
        for h in range(SB_HEADS):
            carry_ref[t, h] = carry[h]
        for m, c in enumerate(pair_cols):
            acc_ref[block_rows(t), c] = acc[m]

    def is_live(carry):
        return (jnp.max(functools.reduce(jnp.maximum, carry)) > SKIP_LOG).astype(jnp.int32)

    query_blocks = range(ATT_QBLOCKS)
    fixed = [(t, t - d, i0 + t - d, blk if d < ATT_FIXED - 1 else blk // 2)
             for t in query_blocks for d in range(ATT_FIXED)]
    states = sweep(fixed, {t: [None] * SB_HEADS for t in query_blocks},
                   {t: [None] * (SB_HEADS // 2) for t in query_blocks}, fill=gate_matmuls)
    gate_matmuls(wgate_ref.shape[1] // MXU_COLS)
    live = []
    for t in query_blocks:
        store_state(t, *states[ATT_FIXED * t + ATT_FIXED - 2])
        carry, acc = states[ATT_FIXED * t + ATT_FIXED - 1]
        live.append(is_live(carry))
        o_ref[block_rows(t), :] = jnp.concatenate(acc, axis=1).astype(BF16)

    for t in query_blocks:
        def body(state, t=t):
            j, _ = state
            carry, acc = sweep([(t, 0, j, blk)], {t: [carry_ref[t, h] for h in range(SB_HEADS)]},
                               {t: [acc_ref[block_rows(t), c] for c in pair_cols]})[-1]
            store_state(t, carry, acc)
            return j - 1, is_live(carry)

        lax.while_loop(lambda s: (s[0] >= 0) & (s[1] > 0), body,
                       (i0 + t - (ATT_FIXED - 1), live[t]))

        @pl.when(live[t] > 0)
        def _(t=t):
            o_ref[block_rows(t), :] = acc_ref[block_rows(t), :].astype(BF16)


def _attention(q, k, v, xn, wgate, l, *, batch, lp, pad):
    steps = lp // (ATT_BLOCK * ATT_QBLOCKS)
    assert lp == steps * ATT_BLOCK * ATT_QBLOCKS
    rows = lambda width: pl.BlockSpec((ATT_QBLOCKS * ATT_BLOCK, width), lambda b, g: (b * steps + g, 0))
    kvspec = pl.BlockSpec((lp, SB_WIDTH), lambda b, g: (b, 0), pipeline_mode=pl.Buffered(1))
    n_gate = wgate.shape[2]
    return pl.pallas_call(
        functools.partial(_attn_body, pad=pad),
        grid=(batch, steps),
        in_specs=[rows(SB_WIDTH), kvspec, kvspec, rows(D_MODEL), _layer(wgate, l)],
        out_specs=[rows(SB_WIDTH), rows(n_gate)],
        out_shape=[jax.ShapeDtypeStruct(q.shape, BF16),
                   jax.ShapeDtypeStruct((q.shape[0], n_gate), F32)],
        scratch_shapes=[pltpu.VMEM((ATT_QBLOCKS * ATT_BLOCK, SB_WIDTH), F32),
                        pltpu.VMEM((ATT_QBLOCKS, SB_HEADS, ATT_BLOCK, ATT_BLOCK), F32)],
        compiler_params=_params(2),
        name="sb_attention",
    )(q, k, v, xn, wgate)


def _mix_body(h_ref, gates_ref, a_ref, u_ref, uprev_ref, cw_ref, cb_ref, lng_ref, lnb_ref,
              wao_ref, wco_ref, wo_ref, o_ref, ubuf_ref, shift_ref, conv_ref, *, tm):
    ubuf_ref[0:CONV_HALO, :] = uprev_ref[...]
    ubuf_ref[CONV_HALO:, :] = u_ref[...]
    base = CONV_HALO - (CONV_K - 1)
    sub = F32_SUBLANES
    for r in range(sub):
        n = tm + sub * ((CONV_K - 1 - r) // sub)
        shift_ref[r, 0:n // sub] = ubuf_ref[base + r:base + r + n, :].reshape(n // sub, sub, CONV_CH)

    def conv_chunk(c0):
        y = jnp.broadcast_to(cb_ref[...], (CONV_CHUNK // sub, sub, CONV_CH))
        for k in range(CONV_K):
            g0 = (c0 + k - k % sub) // sub
            y = y + cw_ref[k] * shift_ref[k % sub, g0:g0 + CONV_CHUNK // sub]
        y = y.reshape(CONV_CHUNK, CONV_CH)
        mu = jnp.mean(y, axis=-1, keepdims=True)
        yc = y - mu
        var = jnp.mean(yc * yc, axis=-1, keepdims=True)
        ln = yc * lax.rsqrt(var + LN_EPS) * lng_ref[...] + lnb_ref[...]
        conv_ref[c0:c0 + CONV_CHUNK, :] = (ln * _sigmoid(ln)).astype(BF16)

    for r0 in range(0, tm, MIX_ROWS):
        rows = slice(r0, r0 + MIX_ROWS)
        for c0 in range(r0, r0 + MIX_ROWS, CONV_CHUNK):
            conv_chunk(c0)
        gate_sb = _sigmoid(gates_ref[rows, :D_MODEL])
        gate_cv = _sigmoid(gates_ref[rows, D_MODEL:])
        y_sb = jnp.dot(a_ref[rows, :], wao_ref[...], preferred_element_type=F32)
        y_cv = jnp.dot(conv_ref[rows, :], wco_ref[...], preferred_element_type=F32)
        m = (gate_sb * y_sb + gate_cv * y_cv).astype(BF16)
        o_ref[rows, :] = h_ref[rows, :] + jnp.dot(m, wo_ref[...], preferred_element_type=F32)


def _mix(h, gates, a, u, cw, cb, lng, lnb, wao, wco, wo, l, *, tm):
    rows = h.shape[0]
    halo_per_tile = tm // CONV_HALO
    row = lambda width: pl.BlockSpec((tm, width), lambda i: (i, 0))
    prev = pl.BlockSpec((CONV_HALO, CONV_CH), lambda i: (jnp.maximum(i * halo_per_tile - 1, 0), 0))
    shift_rows = tm + F32_SUBLANES * ((CONV_K - 1) // F32_SUBLANES)
    return pl.pallas_call(
        functools.partial(_mix_body, tm=tm),
        grid=(rows // tm,),
        in_specs=[row(D_MODEL), row(gates.shape[1]), row(SB_WIDTH),
                  row(CONV_CH), prev, _resident(cw.shape), _resident((1, CONV_CH)),
                  _resident((1, CONV_CH)), _resident((1, CONV_CH)), _layer(wao, l),
                  _layer(wco, l), _layer(wo, l)],
        out_specs=row(D_MODEL),
        out_shape=jax.ShapeDtypeStruct((rows, D_MODEL), F32),
        scratch_shapes=[pltpu.VMEM((CONV_HALO + tm, CONV_CH), F32),
                        pltpu.VMEM((F32_SUBLANES, shift_rows // F32_SUBLANES, F32_SUBLANES, CONV_CH), F32),
                        pltpu.VMEM((tm, CONV_CH), BF16)],
        compiler_params=_params(1),
        name="mixer_out",
    )(h, gates, a, u, u, cw, cb, lng, lnb, wao, wco, wo)


def kernel(x, meta, ffn1_norm, ffn1_w_gate, ffn1_w_up, ffn1_w_down, mix_norm, w_in, w_attn_out,
           conv_w, conv_b, conv_ln_g, conv_ln_b, w_conv_out, w_out, ffn2_norm, ffn2_w_gate,
           ffn2_w_up, ffn2_w_down, final_norm):
    batch, seq, d = x.shape
    depth = w_in.shape[0]
    assert d == D_MODEL and meta.shape == (N_META, D_MODEL) and seq % ATT_BLOCK == 0
    assert N_META <= ATT_BLOCK
    pad = ATT_BLOCK - N_META
    lp = seq + ATT_BLOCK
    tm_ffn = 1040
    tm_mix = 640
    tm_out = 1024
    assert lp % tm_ffn == 0 and lp % tm_mix == 0 and tm_mix % CONV_HALO == 0
    assert tm_mix % CONV_CHUNK == 0 and seq % tm_out == 0

    head = jnp.concatenate([jnp.zeros((pad, d), x.dtype), meta.astype(x.dtype)], axis=0)
    h = x.reshape(batch * seq, d)

    vec = lambda p: p.reshape(1, -1).astype(F32)
    conv_taps = lambda w: jnp.broadcast_to(w.astype(F32)[:, None, :], (CONV_K, F32_SUBLANES, CONV_CH))
    ffn1 = [_to_bf16(w) for w in (ffn1_w_gate, ffn1_w_up, ffn1_w_down)]
    ffn2 = [_to_bf16(w) for w in (ffn2_w_gate, ffn2_w_up, ffn2_w_down)]
    w_proj, w_gates = _to_bf16(w_in, (3 * SB_WIDTH + 2 * CONV_CH, w_in.shape[2]))
    w_ao, w_co, w_o = _to_bf16(w_attn_out), _to_bf16(w_conv_out), _to_bf16(w_out)
    for l in range(depth):
        ends = dict(frames=(lp, seq), head=head) if l == 0 else {}
        h = _ffn(h, vec(ffn1_norm[l]), *ffn1, l, tm=tm_ffn, **ends)
        xn, q, k, v, u = _proj(h, vec(mix_norm[l]), w_proj, l, batch=batch, lp=lp, pad=pad, tm=tm_ffn)
        a, gates = _attention(q, k, v, xn, w_gates, l, batch=batch, lp=lp, pad=pad)
        h = _mix(h, gates, a, u, conv_taps(conv_w[l]), vec(conv_b[l]),
                 vec(conv_ln_g[l]), vec(conv_ln_b[l]), w_ao, w_co, w_o, l, tm=tm_mix)
        if l < depth - 1:
            h = _ffn(h, vec(ffn2_norm[l]), *ffn2, l, tm=tm_ffn)
    out = _ffn(h, vec(ffn2_norm[depth - 1]), *ffn2, depth - 1, vec(final_norm), tm=tm_out,
               frames=(lp, seq))
    return out.reshape(batch, seq, d)
```

```python
import functools
import math

import jax
import jax.numpy as jnp
from jax import lax
from jax.experimental import pallas as pl
from jax.experimental.pallas import tpu as pltpu

D_MODEL = 1024
N_META = 16
SB_HEADS = 8
SB_HEAD_DIM = 64
SB_WIDTH = SB_HEADS * SB_HEAD_DIM
HEAD_PAIR = 2 * SB_HEAD_DIM
CONV_CH = 512
CONV_K = 31
D_FF = 2816
NORM_EPS = 1e-6
LN_EPS = 1e-5
LOG2E = 1.4426950408889634

F32_SUBLANES = 8
ATT_BLOCK = 128
ATT_HEAD_GROUPS = 2
ATT_QBLOCKS = 5
ATT_FIXED = 3
ATT_TAIL_ROWS = 32
MXU_COLS = 256
FF_CHUNK = MXU_COLS
CONV_HALO = 32
CONV_CHUNK = 32
MIX_ROWS = 320
CAST_STEPS = 2
VMEM_LIMIT = 56 * 1024 * 1024

SKIP_LOG = -104.0
MASKED = 1e30

F32 = jnp.float32
BF16 = jnp.bfloat16


def _rmsnorm(x, g):
    return x * lax.rsqrt(jnp.mean(x * x, axis=-1, keepdims=True) + NORM_EPS) * g


def _sigmoid(x):
    return 1.0 / (1.0 + jnp.exp(-x))


def _resident(shape):
    return pl.BlockSpec(shape, lambda *_: (0,) * len(shape), pipeline_mode=pl.Buffered(1))


def _layer(w, l):
    return pl.BlockSpec((None,) + w.shape[1:], lambda *_: (l, 0, 0), pipeline_mode=pl.Buffered(1))


def _params(n_axes):
    return pltpu.CompilerParams(dimension_semantics=("arbitrary",) * n_axes,
                                vmem_limit_bytes=VMEM_LIMIT)


def _cast_body(w_ref, *o_refs, splits):
    c0 = 0
    for o_ref, c1 in zip(o_refs, splits):
        o_ref[...] = w_ref[:, c0:c1].astype(BF16)
        c0 = c1


def _to_bf16(w, splits=None):
    depth, k, n = w.shape
    splits = tuple(splits or (n,))
    widths = [c1 - c0 for c0, c1 in zip((0,) + splits[:-1], splits)]
    rows = k // CAST_STEPS
    assert splits[-1] == n and k % CAST_STEPS == 0 and rows % (2 * F32_SUBLANES) == 0
    out = pl.pallas_call(
        functools.partial(_cast_body, splits=splits),
        grid=(depth, CAST_STEPS),
        in_specs=[pl.BlockSpec((None, rows, n), lambda l, i: (l, i, 0))],
        out_specs=[pl.BlockSpec((None, rows, wd), lambda l, i: (l, i, 0)) for wd in widths],
        out_shape=[jax.ShapeDtypeStruct((depth, k, wd), BF16) for wd in widths],
        compiler_params=_params(2),
        name="to_bf16",
    )(w)
    return out if len(out) > 1 else out[0]


def _ffn_body(h_ref, g_ref, wg_ref, wu_ref, wd_ref, *rest, final, tiles_per_batch):
    if final:
        fg_ref, o_ref, act_ref = rest
    elif tiles_per_batch:
        head_ref, o_ref, act_ref = rest
    else:
        o_ref, act_ref = rest
    x = h_ref[...]
    if tiles_per_batch and not final:
        n_head = head_ref.shape[0]
        first = jnp.concatenate([head_ref[...], x[:x.shape[0] - n_head]], axis=0)
        x = jnp.where(pl.program_id(0) % tiles_per_batch == 0, first, x)
    xn = _rmsnorm(x, g_ref[...]).astype(BF16)
    for c0 in range(0, D_FF, FF_CHUNK):
        c1 = min(c0 + FF_CHUNK, D_FF)
        gate = jnp.dot(xn, wg_ref[:, c0:c1], preferred_element_type=F32)
        up = jnp.dot(xn, wu_ref[:, c0:c1], preferred_element_type=F32)
        act_ref[:, c0:c1] = (gate * _sigmoid(gate) * up).astype(BF16)
    y = x + 0.5 * jnp.dot(act_ref[...], wd_ref[...], preferred_element_type=F32)
    if final:
        y = _rmsnorm(y, fg_ref[...])
    o_ref[...] = y


def _ffn(h, g, wg, wu, wd, l, final_g=None, *, tm, frames=None, head=None):
    rows = h.shape[0]
    final = final_g is not None
    row_spec = pl.BlockSpec((tm, D_MODEL), lambda i: (i, 0))
    in_spec = row_spec
    tiles_per_batch = None
    if frames is not None:
        lp, seq = frames
        if final:
            tiles_per_batch = seq // tm
            rows = rows // lp * seq
            start = lambda i: i // tiles_per_batch * lp + (lp - seq) + i % tiles_per_batch * tm
        else:
            tiles_per_batch = lp // tm
            rows = rows // seq * lp
            start = lambda i: (i // tiles_per_batch * seq
                               + jnp.maximum(i % tiles_per_batch * tm - (lp - seq), 0))
        in_spec = pl.BlockSpec((pl.Element(tm), pl.Element(D_MODEL)),
                               lambda i: (pl.multiple_of(start(i), F32_SUBLANES), 0))
    in_specs = [in_spec, _resident((1, D_MODEL)), _layer(wg, l), _layer(wu, l), _layer(wd, l)]
    args = [h, g, wg, wu, wd]
    if final:
        in_specs.append(_resident((1, D_MODEL)))
        args.append(final_g)
    elif head is not None:
        in_specs.append(_resident(head.shape))
        args.append(head)
    return pl.pallas_call(
        functools.partial(_ffn_body, final=final, tiles_per_batch=tiles_per_batch),
        grid=(rows // tm,),
        in_specs=in_specs,
        out_specs=row_spec,
        out_shape=jax.ShapeDtypeStruct((rows, D_MODEL), F32),
        scratch_shapes=[pltpu.VMEM((tm, D_FF), BF16)],
        compiler_params=_params(1),
        name="ffn_final" if final else "ffn",
    )(*args)


def _proj_body(h_ref, g_ref, w_ref, q_ref, k_ref, v_ref, u_ref, *, tm, pad):
    xn = _rmsnorm(h_ref[...], g_ref[...]).astype(BF16)

    def col(i):
        return jnp.dot(xn, w_ref[:, i * SB_WIDTH:(i + 1) * SB_WIDTH], preferred_element_type=F32)

    q_ref[...] = (col(0) * (-1.0 / math.sqrt(SB_HEAD_DIM))).astype(BF16)
    k_ref[...] = col(1).astype(BF16)
    v_ref[...] = col(2).astype(BF16)
    u = col(3) * _sigmoid(col(4))
    pos = pl.program_id(1) * tm + lax.broadcasted_iota(jnp.int32, (tm, 1), 0)
    u_ref[...] = jnp.where(pos >= pad, u, 0.0)


def _proj(h, g, w, l, *, batch, lp, pad, tm):
    rows = h.shape[0]
    nt = lp // tm
    row = lambda width: pl.BlockSpec((tm, width), lambda b, j: (b * nt + j, 0))
    bf = jax.ShapeDtypeStruct((rows, SB_WIDTH), BF16)
    return pl.pallas_call(
        functools.partial(_proj_body, tm=tm, pad=pad),
        grid=(batch, nt),
        in_specs=[row(D_MODEL), _resident((1, D_MODEL)), _layer(w, l)],
        out_specs=[row(SB_WIDTH), row(SB_WIDTH), row(SB_WIDTH), row(CONV_CH)],
        out_shape=[bf, bf, bf, jax.ShapeDtypeStruct((rows, CONV_CH), F32)],
        compiler_params=_params(2),
        name="mixer_proj",
    )(h, g, w)


def _attn_body(q_ref, k_ref, v_ref, o_ref, acc_ref, carry_ref, *, pad):
    blk = ATT_BLOCK
    i0 = pl.program_id(1) * ATT_QBLOCKS
    row = lax.broadcasted_iota(jnp.int32, (blk, blk), 0)
    lane = lax.broadcasted_iota(jnp.int32, (blk, blk), 1)
    half = jnp.concatenate([(row > lane).astype(BF16), jnp.ones((blk, blk), BF16)], axis=1)
    suffix_and_total = jnp.concatenate([half, half], axis=0)
    first = (lane < SB_HEAD_DIM).astype(BF16)
    second = (lane >= SB_HEAD_DIM).astype(BF16)
    nt = (((1,), (1,)), ((), ()))

    pair_cols = [slice(m * HEAD_PAIR, (m + 1) * HEAD_PAIR) for m in range(SB_HEADS // 2)]
    head_groups = [range(g, g + SB_HEADS // ATT_HEAD_GROUPS)
                   for g in range(0, SB_HEADS, SB_HEADS // ATT_HEAD_GROUPS)]
    block_rows = lambda t: slice(t * blk, (t + 1) * blk)

    def add_rows(old, new):
        if old is None:
            assert new.shape[0] == blk
            return new
        nr = new.shape[0]
        top = old[:nr] + new
        return top if nr == blk else jnp.concatenate([top, old[nr:]], axis=0)

    def sweep(units, carry, acc):
        keys, masked = {}, []
        for t, key, j, nr in units:
            if key not in keys:
                start = pl.multiple_of(jnp.maximum(j, 0) * blk, blk)
                keys[key] = (start, j, k_ref[pl.ds(start, blk), :], v_ref[pl.ds(start, blk), :])
            start, jj = keys[key][:2]
            valid = (start + lane < (i0 + t) * blk + row) & (start + lane >= pad) & (jj >= 0)
            masked.append(jnp.where(valid, 0.0, MASKED)[:nr])
        visitors = {key: [u for u, unit in enumerate(units) if unit[1] == key] for key in keys}
        q_pair = {t: [(q_ref[block_rows(t), c] * first, q_ref[block_rows(t), c] * second)
                      for c in pair_cols] for t in {unit[0] for unit in units}}

        def per_visitor(key, m, stacked):
            out, r0 = {}, 0
            for u in visitors[key]:
                nr = units[u][3]
                out[u, m] = stacked[r0:r0 + 2 * nr]
                r0 += 2 * nr
            return out

        zneg = {}
        for key, (_, _, kt, _) in keys.items():
            for m, c in enumerate(pair_cols):
                lhs = [q[:units[u][3]] for u in visitors[key] for q in q_pair[units[u][0]][m]]
                zneg.update(per_visitor(key, m, lax.dot_general(
                    jnp.concatenate(lhs, axis=0), kt[:, c], nt, preferred_element_type=F32)))
        log_beta, sums = {}, {}
        for heads in head_groups:
            split = []
            for u, (_, _, _, nr) in enumerate(units):
                for h in heads:
                    zn = zneg[u, h // 2][(h % 2) * nr:(h % 2 + 1) * nr] + masked[u]
                    lk = jnp.minimum(zn, 0.0) - jnp.log(1.0 + jnp.exp2(jnp.abs(zn) * (-LOG2E)))
                    hi = lk.astype(BF16)
                    lo = (lk - hi.astype(F32)).astype(BF16)
                    log_beta[u, h] = lk - zn
                    split.append(jnp.concatenate([hi, lo], axis=1))
            group_sums = jnp.dot(jnp.concatenate(split, axis=0), suffix_and_total,
                                 preferred_element_type=F32)
            r0 = 0
            for u, (_, _, _, nr) in enumerate(units):
                for h in heads:
                    sums[u, h] = group_sums[r0:r0 + nr]
                    r0 += nr

        carry = {t: list(c) for t, c in carry.items()}
        acc = {t: list(a) for t, a in acc.items()}
        probs, carry_after = {}, []
        for u, (t, _, _, nr) in enumerate(units):
            for h in range(SB_HEADS):
                after = sums[u, h][:, :blk]
                if carry[t][h] is not None:
                    after = after + carry[t][h][:nr]
                probs[u, h] = jnp.exp(log_beta[u, h] + after).astype(BF16)
                carry[t][h] = add_rows(carry[t][h], sums[u, h][:, blk:])
            carry_after.append(list(carry[t]))
        outs = {}
        for key, (_, _, _, vt) in keys.items():
            for m, c in enumerate(pair_cols):
                lhs = [probs[u, 2 * m + s] for u in visitors[key] for s in range(2)]
                outs.update(per_visitor(key, m, jnp.dot(jnp.concatenate(lhs, axis=0), vt[:, c],
                                                        preferred_element_type=F32)))
        states = []
        for u, (t, _, _, nr) in enumerate(units):
            own = lax.broadcasted_iota(jnp.int32, (nr, HEAD_PAIR), 1) < SB_HEAD_DIM
            for m in range(SB_HEADS // 2):
                acc[t][m] = add_rows(acc[t][m], jnp.where(own, outs[u, m][:nr], outs[u, m][nr:]))
            states.append((carry_after[u], list(acc[t])))
        return states

    def store_state(t, carry, acc):
        for h in range(SB_HEADS):
            carry_ref[t, h] = carry[h]
        for m, c in enumerate(pair_cols):
            acc_ref[block_rows(t), c] = acc[m]

    def is_live(carry):
        return (jnp.max(functools.reduce(jnp.maximum, carry)) > SKIP_LOG).astype(jnp.int32)

    query_blocks = range(ATT_QBLOCKS)
    fixed = [(t, t - d, i0 + t - d, blk if d < ATT_FIXED - 1 else ATT_TAIL_ROWS)
             for t in query_blocks for d in range(ATT_FIXED)]
    states = sweep(fixed, {t: [None] * SB_HEADS for t in query_blocks},
                   {t: [None] * (SB_HEADS // 2) for t in query_blocks})
    live = []
    for t in query_blocks:
        store_state(t, *states[ATT_FIXED * t + ATT_FIXED - 2])
        carry, acc = states[ATT_FIXED * t + ATT_FIXED - 1]
        live.append(is_live(carry))
        o_ref[block_rows(t), :] = jnp.concatenate(acc, axis=1).astype(BF16)

    for t in query_blocks:
        def body(state, t=t):
            j, _ = state
            carry, acc = sweep([(t, 0, j, blk)], {t: [carry_ref[t, h] for h in range(SB_HEADS)]},
                               {t: [acc_ref[block_rows(t), c] for c in pair_cols]})[-1]
            store_state(t, carry, acc)
            return j - 1, is_live(carry)

        lax.while_loop(lambda s: (s[0] >= 0) & (s[1] > 0), body,
                       (i0 + t - (ATT_FIXED - 1), live[t]))

        @pl.when(live[t] > 0)
        def _(t=t):
            o_ref[block_rows(t), :] = acc_ref[block_rows(t), :].astype(BF16)


def _attention(q, k, v, *, batch, lp, pad):
    steps = lp // (ATT_BLOCK * ATT_QBLOCKS)
    assert lp == steps * ATT_BLOCK * ATT_QBLOCKS
    qspec = pl.BlockSpec((ATT_QBLOCKS * ATT_BLOCK, SB_WIDTH), lambda b, g: (b * steps + g, 0))
    kvspec = pl.BlockSpec((lp, SB_WIDTH), lambda b, g: (b, 0))
    return pl.pallas_call(
        functools.partial(_attn_body, pad=pad),
        grid=(batch, steps),
        in_specs=[qspec, kvspec, kvspec],
        out_specs=qspec,
        out_shape=jax.ShapeDtypeStruct(q.shape, BF16),
        scratch_shapes=[pltpu.VMEM((ATT_QBLOCKS * ATT_BLOCK, SB_WIDTH), F32),
                        pltpu.VMEM((ATT_QBLOCKS, SB_HEADS, ATT_BLOCK, ATT_BLOCK), F32)],
        compiler_params=_params(2),
        name="sb_attention",
    )(q, k, v)


def _mix_body(h_ref, g_ref, wgate_ref, a_ref, u_ref, uprev_ref, cw_ref, cb_ref, lng_ref, lnb_ref,
              wao_ref, wco_ref, wo_ref, o_ref, ubuf_ref, shift_ref, conv_ref, *, tm):
    ubuf_ref[0:CONV_HALO, :] = uprev_ref[...]
    ubuf_ref[CONV_HALO:, :] = u_ref[...]
    base = CONV_HALO - (CONV_K - 1)
    sub = F32_SUBLANES
    for r in range(sub):
        n = tm + sub * ((CONV_K - 1 - r) // sub)
        shift_ref[r, 0:n // sub] = ubuf_ref[base + r:base + r + n, :].reshape(n // sub, sub, CONV_CH)

    def conv_chunk(c0):
        y = jnp.broadcast_to(cb_ref[...], (CONV_CHUNK // sub, sub, CONV_CH))
        for k in range(CONV_K):
            g0 = (c0 + k - k % sub) // sub
            y = y + cw_ref[k] * shift_ref[k % sub, g0:g0 + CONV_CHUNK // sub]
        y = y.reshape(CONV_CHUNK, CONV_CH)
        mu = jnp.mean(y, axis=-1, keepdims=True)
        yc = y - mu
        var = jnp.mean(yc * yc, axis=-1, keepdims=True)
        ln = yc * lax.rsqrt(var + LN_EPS) * lng_ref[...] + lnb_ref[...]
        conv_ref[c0:c0 + CONV_CHUNK, :] = (ln * _sigmoid(ln)).astype(BF16)

    for r0 in range(0, tm, MIX_ROWS):
        rows = slice(r0, r0 + MIX_ROWS)
        for c0 in range(r0, r0 + MIX_ROWS, CONV_CHUNK):
            conv_chunk(c0)
        x = h_ref[rows, :]
        xn = _rmsnorm(x, g_ref[...]).astype(BF16)
        gate_sb = _sigmoid(jnp.dot(xn, wgate_ref[:, :D_MODEL], preferred_element_type=F32))
        gate_cv = _sigmoid(jnp.dot(xn, wgate_ref[:, D_MODEL:], preferred_element_type=F32))
        y_sb = jnp.dot(a_ref[rows, :], wao_ref[...], preferred_element_type=F32)
        y_cv = jnp.dot(conv_ref[rows, :], wco_ref[...], preferred_element_type=F32)
        m = (gate_sb * y_sb + gate_cv * y_cv).astype(BF16)
        o_ref[rows, :] = x + jnp.dot(m, wo_ref[...], preferred_element_type=F32)


def _mix(h, g, wgate, a, u, cw, cb, lng, lnb, wao, wco, wo, l, *, tm):
    rows = h.shape[0]
    halo_per_tile = tm // CONV_HALO
    row = lambda width: pl.BlockSpec((tm, width), lambda i: (i, 0))
    prev = pl.BlockSpec((CONV_HALO, CONV_CH), lambda i: (jnp.maximum(i * halo_per_tile - 1, 0), 0))
    shift_rows = tm + F32_SUBLANES * ((CONV_K - 1) // F32_SUBLANES)
    return pl.pallas_call(
        functools.partial(_mix_body, tm=tm),
        grid=(rows // tm,),
        in_specs=[row(D_MODEL), _resident((1, D_MODEL)), _layer(wgate, l), row(SB_WIDTH),
                  row(CONV_CH), prev, _resident(cw.shape), _resident((1, CONV_CH)),
                  _resident((1, CONV_CH)), _resident((1, CONV_CH)), _layer(wao, l),
                  _layer(wco, l), _layer(wo, l)],
        out_specs=row(D_MODEL),
        out_shape=jax.ShapeDtypeStruct((rows, D_MODEL), F32),
        scratch_shapes=[pltpu.VMEM((CONV_HALO + tm, CONV_CH), F32),
                        pltpu.VMEM((F32_SUBLANES, shift_rows // F32_SUBLANES, F32_SUBLANES, CONV_CH), F32),
                        pltpu.VMEM((tm, CONV_CH), BF16)],
        compiler_params=_params(1),
        name="mixer_out",
    )(h, g, wgate, a, u, u, cw, cb, lng, lnb, wao, wco, wo)


def kernel(x, meta, ffn1_norm, ffn1_w_gate, ffn1_w_up, ffn1_w_down, mix_norm, w_in, w_attn_out,
           conv_w, conv_b, conv_ln_g, conv_ln_b, w_conv_out, w_out, ffn2_norm, ffn2_w_gate,
           ffn2_w_up, ffn2_w_down, final_norm):
    batch, seq, d = x.shape
    depth = w_in.shape[0]
    assert d == D_MODEL and meta.shape == (N_META, D_MODEL) and seq % ATT_BLOCK == 0
    assert N_META <= ATT_BLOCK
    pad = ATT_BLOCK - N_META
    lp = seq + ATT_BLOCK
    tm_ffn = 1040
    tm_mix = 640
    tm_out = 1024
    assert lp % tm_ffn == 0 and lp % tm_mix == 0 and tm_mix % CONV_HALO == 0
    assert tm_mix % CONV_CHUNK == 0 and seq % tm_out == 0

    head = jnp.concatenate([jnp.zeros((pad, d), x.dtype), meta.astype(x.dtype)], axis=0)
    h = x.reshape(batch * seq, d)

    vec = lambda p: p.reshape(1, -1).astype(F32)
    conv_taps = lambda w: jnp.broadcast_to(w.astype(F32)[:, None, :], (CONV_K, F32_SUBLANES, CONV_CH))
    ffn1 = [_to_bf16(w) for w in (ffn1_w_gate, ffn1_w_up, ffn1_w_down)]
    ffn2 = [_to_bf16(w) for w in (ffn2_w_gate, ffn2_w_up, ffn2_w_down)]
    w_proj, w_gates = _to_bf16(w_in, (3 * SB_WIDTH + 2 * CONV_CH, w_in.shape[2]))
    w_ao, w_co, w_o = _to_bf16(w_attn_out), _to_bf16(w_conv_out), _to_bf16(w_out)
    for l in range(depth):
        ends = dict(frames=(lp, seq), head=head) if l == 0 else {}
        h = _ffn(h, vec(ffn1_norm[l]), *ffn1, l, tm=tm_ffn, **ends)
        q, k, v, u = _proj(h, vec(mix_norm[l]), w_proj, l, batch=batch, lp=lp, pad=pad, tm=tm_ffn)
        a = _attention(q, k, v, batch=batch, lp=lp, pad=pad)
        h = _mix(h, vec(mix_norm[l]), w_gates, a, u, conv_taps(conv_w[l]), vec(conv_b[l]),
                 vec(conv_ln_g[l]), vec(conv_ln_b[l]), w_ao, w_co, w_o, l, tm=tm_mix)
        if l < depth - 1:
            h = _ffn(h, vec(ffn2_norm[l]), *ffn2, l, tm=tm_ffn)
    out = _ffn(h, vec(ffn2_norm[depth - 1]), *ffn2, depth - 1, vec(final_norm), tm=tm_out,
               frames=(lp, seq))
    return out.reshape(batch, seq, d)
```

```python
import functools
import math

import jax
import jax.numpy as jnp
from jax import lax
from jax.experimental import pallas as pl
from jax.experimental.pallas import tpu as pltpu

D_MODEL = 1024
N_META = 16
SB_HEADS = 8
SB_HEAD_DIM = 64
SB_WIDTH = SB_HEADS * SB_HEAD_DIM
HEAD_PAIR = 2 * SB_HEAD_DIM
CONV_CH = 512
CONV_K = 31
D_FF = 2816
NORM_EPS = 1e-6
LN_EPS = 1e-5
LOG2E = 1.4426950408889634

F32_SUBLANES = 8
ATT_BLOCK = 128
ATT_HEAD_GROUPS = 2
ATT_QBLOCKS = 5
ATT_FIXED = 3
ATT_TAIL_ROWS = 48
MXU_COLS = 256
FF_CHUNK = MXU_COLS
CONV_HALO = 32
CONV_CHUNK = 32
MIX_ROWS = 320
CAST_STEPS = 2
VMEM_LIMIT = 56 * 1024 * 1024

SKIP_LOG = -104.0
MASKED = 1e30

F32 = jnp.float32
BF16 = jnp.bfloat16


def _rmsnorm(x, g):
    return x * lax.rsqrt(jnp.mean(x * x, axis=-1, keepdims=True) + NORM_EPS) * g


def _sigmoid(x):
    return 1.0 / (1.0 + jnp.exp(-x))


def _resident(shape):
    return pl.BlockSpec(shape, lambda *_: (0,) * len(shape), pipeline_mode=pl.Buffered(1))


def _layer(w, l):
    return pl.BlockSpec((None,) + w.shape[1:], lambda *_: (l, 0, 0), pipeline_mode=pl.Buffered(1))


def _params(n_axes):
    return pltpu.CompilerParams(dimension_semantics=("arbitrary",) * n_axes,
                                vmem_limit_bytes=VMEM_LIMIT)


def _cast_body(w_ref, *o_refs, splits):
    c0 = 0
    for o_ref, c1 in zip(o_refs, splits):
        o_ref[...] = w_ref[:, c0:c1].astype(BF16)
        c0 = c1


def _to_bf16(w, splits=None):
    depth, k, n = w.shape
    splits = tuple(splits or (n,))
    widths = [c1 - c0 for c0, c1 in zip((0,) + splits[:-1], splits)]
    rows = k // CAST_STEPS
    assert splits[-1] == n and k % CAST_STEPS == 0 and rows % (2 * F32_SUBLANES) == 0
    out = pl.pallas_call(
        functools.partial(_cast_body, splits=splits),
        grid=(depth, CAST_STEPS),
        in_specs=[pl.BlockSpec((None, rows, n), lambda l, i: (l, i, 0))],
        out_specs=[pl.BlockSpec((None, rows, wd), lambda l, i: (l, i, 0)) for wd in widths],
        out_shape=[jax.ShapeDtypeStruct((depth, k, wd), BF16) for wd in widths],
        compiler_params=_params(2),
        name="to_bf16",
    )(w)
    return out if len(out) > 1 else out[0]


def _ffn_body(h_ref, g_ref, wg_ref, wu_ref, wd_ref, *rest, final, tiles_per_batch):
    if final:
        fg_ref, o_ref, act_ref = rest
    elif tiles_per_batch:
        head_ref, o_ref, act_ref = rest
    else:
        o_ref, act_ref = rest
    x = h_ref[...]
    if tiles_per_batch and not final:
        n_head = head_ref.shape[0]
        first = jnp.concatenate([head_ref[...], x[:x.shape[0] - n_head]], axis=0)
        x = jnp.where(pl.program_id(0) % tiles_per_batch == 0, first, x)
    xn = _rmsnorm(x, g_ref[...]).astype(BF16)
    for c0 in range(0, D_FF, FF_CHUNK):
        c1 = min(c0 + FF_CHUNK, D_FF)
        gate = jnp.dot(xn, wg_ref[:, c0:c1], preferred_element_type=F32)
        up = jnp.dot(xn, wu_ref[:, c0:c1], preferred_element_type=F32)
        act_ref[:, c0:c1] = (gate * _sigmoid(gate) * up).astype(BF16)
    y = x + 0.5 * jnp.dot(act_ref[...], wd_ref[...], preferred_element_type=F32)
    if final:
        y = _rmsnorm(y, fg_ref[...])
    o_ref[...] = y


def _ffn(h, g, wg, wu, wd, l, final_g=None, *, tm, frames=None, head=None):
    rows = h.shape[0]
    final = final_g is not None
    row_spec = pl.BlockSpec((tm, D_MODEL), lambda i: (i, 0))
    in_spec = row_spec
    tiles_per_batch = None
    if frames is not None:
        lp, seq = frames
        if final:
            tiles_per_batch = seq // tm
            rows = rows // lp * seq
            start = lambda i: i // tiles_per_batch * lp + (lp - seq) + i % tiles_per_batch * tm
        else:
            tiles_per_batch = lp // tm
            rows = rows // seq * lp
            start = lambda i: (i // tiles_per_batch * seq
                               + jnp.maximum(i % tiles_per_batch * tm - (lp - seq), 0))
        in_spec = pl.BlockSpec((pl.Element(tm), pl.Element(D_MODEL)),
                               lambda i: (pl.multiple_of(start(i), F32_SUBLANES), 0))
    in_specs = [in_spec, _resident((1, D_MODEL)), _layer(wg, l), _layer(wu, l), _layer(wd, l)]
    args = [h, g, wg, wu, wd]
    if final:
        in_specs.append(_resident((1, D_MODEL)))
        args.append(final_g)
    elif head is not None:
        in_specs.append(_resident(head.shape))
        args.append(head)
    return pl.pallas_call(
        functools.partial(_ffn_body, final=final, tiles_per_batch=tiles_per_batch),
        grid=(rows // tm,),
        in_specs=in_specs,
        out_specs=row_spec,
        out_shape=jax.ShapeDtypeStruct((rows, D_MODEL), F32),
        scratch_shapes=[pltpu.VMEM((tm, D_FF), BF16)],
        compiler_params=_params(1),
        name="ffn_final" if final else "ffn",
    )(*args)


def _proj_body(h_ref, g_ref, w_ref, q_ref, k_ref, v_ref, u_ref, *, tm, pad):
    xn = _rmsnorm(h_ref[...], g_ref[...]).astype(BF16)

    def col(i):
        return jnp.dot(xn, w_ref[:, i * SB_WIDTH:(i + 1) * SB_WIDTH], preferred_element_type=F32)

    q_ref[...] = (col(0) * (-1.0 / math.sqrt(SB_HEAD_DIM))).astype(BF16)
    k_ref[...] = col(1).astype(BF16)
    v_ref[...] = col(2).astype(BF16)
    u = col(3) * _sigmoid(col(4))
    pos = pl.program_id(1) * tm + lax.broadcasted_iota(jnp.int32, (tm, 1), 0)
    u_ref[...] = jnp.where(pos >= pad, u, 0.0)


def _proj(h, g, w, l, *, batch, lp, pad, tm):
    rows = h.shape[0]
    nt = lp // tm
    row = lambda width: pl.BlockSpec((tm, width), lambda b, j: (b * nt + j, 0))
    bf = jax.ShapeDtypeStruct((rows, SB_WIDTH), BF16)
    return pl.pallas_call(
        functools.partial(_proj_body, tm=tm, pad=pad),
        grid=(batch, nt),
        in_specs=[row(D_MODEL), _resident((1, D_MODEL)), _layer(w, l)],
        out_specs=[row(SB_WIDTH), row(SB_WIDTH), row(SB_WIDTH), row(CONV_CH)],
        out_shape=[bf, bf, bf, jax.ShapeDtypeStruct((rows, CONV_CH), F32)],
        compiler_params=_params(2),
        name="mixer_proj",
    )(h, g, w)


def _attn_body(q_ref, k_ref, v_ref, o_ref, acc_ref, carry_ref, *, pad):
    blk = ATT_BLOCK
    i0 = pl.program_id(1) * ATT_QBLOCKS
    row = lax.broadcasted_iota(jnp.int32, (blk, blk), 0)
    lane = lax.broadcasted_iota(jnp.int32, (blk, blk), 1)
    half = jnp.concatenate([(row > lane).astype(BF16), jnp.ones((blk, blk), BF16)], axis=1)
    suffix_and_total = jnp.concatenate([half, half], axis=0)
    first = (lane < SB_HEAD_DIM).astype(BF16)
    second = (lane >= SB_HEAD_DIM).astype(BF16)
    nt = (((1,), (1,)), ((), ()))

    pair_cols = [slice(m * HEAD_PAIR, (m + 1) * HEAD_PAIR) for m in range(SB_HEADS // 2)]
    head_groups = [range(g, g + SB_HEADS // ATT_HEAD_GROUPS)
                   for g in range(0, SB_HEADS, SB_HEADS // ATT_HEAD_GROUPS)]
    block_rows = lambda t: slice(t * blk, (t + 1) * blk)

    def add_rows(old, new):
        if old is None:
            assert new.shape[0] == blk
            return new
        nr = new.shape[0]
        top = old[:nr] + new
        return top if nr == blk else jnp.concatenate([top, old[nr:]], axis=0)

    def sweep(units, carry, acc):
        keys, masked = {}, []
        for t, key, j, nr in units:
            if key not in keys:
                start = pl.multiple_of(jnp.maximum(j, 0) * blk, blk)
                keys[key] = (start, j, k_ref[pl.ds(start, blk), :], v_ref[pl.ds(start, blk), :])
            start, jj = keys[key][:2]
            valid = (start + lane < (i0 + t) * blk + row) & (start + lane >= pad) & (jj >= 0)
            masked.append(jnp.where(valid, 0.0, MASKED)[:nr])
        visitors = {key: [u for u, unit in enumerate(units) if unit[1] == key] for key in keys}
        q_pair = {t: [(q_ref[block_rows(t), c] * first, q_ref[block_rows(t), c] * second)
                      for c in pair_cols] for t in {unit[0] for unit in units}}

        def per_visitor(key, m, stacked):
            out, r0 = {}, 0
            for u in visitors[key]:
                nr = units[u][3]
                out[u, m] = stacked[r0:r0 + 2 * nr]
                r0 += 2 * nr
            return out

        zneg = {}
        for key, (_, _, kt, _) in keys.items():
            for m, c in enumerate(pair_cols):
                lhs = [q[:units[u][3]] for u in visitors[key] for q in q_pair[units[u][0]][m]]
                zneg.update(per_visitor(key, m, lax.dot_general(
                    jnp.concatenate(lhs, axis=0), kt[:, c], nt, preferred_element_type=F32)))
        log_beta, sums = {}, {}
        for heads in head_groups:
            split = []
            for u, (_, _, _, nr) in enumerate(units):
                for h in heads:
                    zn = zneg[u, h // 2][(h % 2) * nr:(h % 2 + 1) * nr] + masked[u]
                    lk = jnp.minimum(zn, 0.0) - jnp.log(1.0 + jnp.exp2(jnp.abs(zn) * (-LOG2E)))
                    hi = lk.astype(BF16)
                    lo = (lk - hi.astype(F32)).astype(BF16)
                    log_beta[u, h] = lk - zn
                    split.append(jnp.concatenate([hi, lo], axis=1))
            group_sums = jnp.dot(jnp.concatenate(split, axis=0), suffix_and_total,
                                 preferred_element_type=F32)
            r0 = 0
            for u, (_, _, _, nr) in enumerate(units):
                for h in heads:
                    sums[u, h] = group_sums[r0:r0 + nr]
                    r0 += nr

        carry = {t: list(c) for t, c in carry.items()}
        acc = {t: list(a) for t, a in acc.items()}
        probs, carry_after = {}, []
        for u, (t, _, _, nr) in enumerate(units):
            for h in range(SB_HEADS):
                after = sums[u, h][:, :blk]
                if carry[t][h] is not None:
                    after = after + carry[t][h][:nr]
                probs[u, h] = jnp.exp(log_beta[u, h] + after).astype(BF16)
                carry[t][h] = add_rows(carry[t][h], sums[u, h][:, blk:])
            carry_after.append(list(carry[t]))
        outs = {}
        for key, (_, _, _, vt) in keys.items():
            for m, c in enumerate(pair_cols):
                lhs = [probs[u, 2 * m + s] for u in visitors[key] for s in range(2)]
                outs.update(per_visitor(key, m, jnp.dot(jnp.concatenate(lhs, axis=0), vt[:, c],
                                                        preferred_element_type=F32)))
        states = []
        for u, (t, _, _, nr) in enumerate(units):
            own = lax.broadcasted_iota(jnp.int32, (nr, HEAD_PAIR), 1) < SB_HEAD_DIM
            for m in range(SB_HEADS // 2):
                acc[t][m] = add_rows(acc[t][m], jnp.where(own, outs[u, m][:nr], outs[u, m][nr:]))
            states.append((carry_after[u], list(acc[t])))
        return states

    def store_state(t, carry, acc):
        for h in range(SB_HEADS):
            carry_ref[t, h] = carry[h]
        for m, c in enumerate(pair_cols):
            acc_ref[block_rows(t), c] = acc[m]

    def is_live(carry):
        return (jnp.max(functools.reduce(jnp.maximum, carry)) > SKIP_LOG).astype(jnp.int32)

    query_blocks = range(ATT_QBLOCKS)
    fixed = [(t, t - d, i0 + t - d, blk if d < ATT_FIXED - 1 else ATT_TAIL_ROWS)
             for t in query_blocks for d in range(ATT_FIXED)]
    states = sweep(fixed, {t: [None] * SB_HEADS for t in query_blocks},
                   {t: [None] * (SB_HEADS // 2) for t in query_blocks})
    live = []
    for t in query_blocks:
        store_state(t, *states[ATT_FIXED * t + ATT_FIXED - 2])
        carry, acc = states[ATT_FIXED * t + ATT_FIXED - 1]
        live.append(is_live(carry))
        o_ref[block_rows(t), :] = jnp.concatenate(acc, axis=1).astype(BF16)

    for t in query_blocks:
        def body(state, t=t):
            j, _ = state
            carry, acc = sweep([(t, 0, j, blk)], {t: [carry_ref[t, h] for h in range(SB_HEADS)]},
                               {t: [acc_ref[block_rows(t), c] for c in pair_cols]})[-1]
            store_state(t, carry, acc)
            return j - 1, is_live(carry)

        lax.while_loop(lambda s: (s[0] >= 0) & (s[1] > 0), body,
                       (i0 + t - (ATT_FIXED - 1), live[t]))

        @pl.when(live[t] > 0)
        def _(t=t):
            o_ref[block_rows(t), :] = acc_ref[block_rows(t), :].astype(BF16)


def _attention(q, k, v, *, batch, lp, pad):
    steps = lp // (ATT_BLOCK * ATT_QBLOCKS)
    assert lp == steps * ATT_BLOCK * ATT_QBLOCKS
    qspec = pl.BlockSpec((ATT_QBLOCKS * ATT_BLOCK, SB_WIDTH), lambda b, g: (b * steps + g, 0))
    kvspec = pl.BlockSpec((lp, SB_WIDTH), lambda b, g: (b, 0))
    return pl.pallas_call(
        functools.partial(_attn_body, pad=pad),
        grid=(batch, steps),
        in_specs=[qspec, kvspec, kvspec],
        out_specs=qspec,
        out_shape=jax.ShapeDtypeStruct(q.shape, BF16),
        scratch_shapes=[pltpu.VMEM((ATT_QBLOCKS * ATT_BLOCK, SB_WIDTH), F32),
                        pltpu.VMEM((ATT_QBLOCKS, SB_HEADS, ATT_BLOCK, ATT_BLOCK), F32)],
        compiler_params=_params(2),
        name="sb_attention",
    )(q, k, v)


def _mix_body(h_ref, g_ref, wgate_ref, a_ref, u_ref, uprev_ref, cw_ref, cb_ref, lng_ref, lnb_ref,
              wao_ref, wco_ref, wo_ref, o_ref, ubuf_ref, shift_ref, conv_ref, *, tm):
    ubuf_ref[0:CONV_HALO, :] = uprev_ref[...]
    ubuf_ref[CONV_HALO:, :] = u_ref[...]
    base = CONV_HALO - (CONV_K - 1)
    sub = F32_SUBLANES
    for r in range(sub):
        n = tm + sub * ((CONV_K - 1 - r) // sub)
        shift_ref[r, 0:n // sub] = ubuf_ref[base + r:base + r + n, :].reshape(n // sub, sub, CONV_CH)

    def conv_chunk(c0):
        y = jnp.broadcast_to(cb_ref[...], (CONV_CHUNK // sub, sub, CONV_CH))
        for k in range(CONV_K):
            g0 = (c0 + k - k % sub) // sub
            y = y + cw_ref[k] * shift_ref[k % sub, g0:g0 + CONV_CHUNK // sub]
        y = y.reshape(CONV_CHUNK, CONV_CH)
        mu = jnp.mean(y, axis=-1, keepdims=True)
        yc = y - mu
        var = jnp.mean(yc * yc, axis=-1, keepdims=True)
        ln = yc * lax.rsqrt(var + LN_EPS) * lng_ref[...] + lnb_ref[...]
        conv_ref[c0:c0 + CONV_CHUNK, :] = (ln * _sigmoid(ln)).astype(BF16)

    for r0 in range(0, tm, MIX_ROWS):
        rows = slice(r0, r0 + MIX_ROWS)
        for c0 in range(r0, r0 + MIX_ROWS, CONV_CHUNK):
            conv_chunk(c0)
        x = h_ref[rows, :]
        xn = _rmsnorm(x, g_ref[...]).astype(BF16)
        gate_sb = _sigmoid(jnp.dot(xn, wgate_ref[:, :D_MODEL], preferred_element_type=F32))
        gate_cv = _sigmoid(jnp.dot(xn, wgate_ref[:, D_MODEL:], preferred_element_type=F32))
        y_sb = jnp.dot(a_ref[rows, :], wao_ref[...], preferred_element_type=F32)
        y_cv = jnp.dot(conv_ref[rows, :], wco_ref[...], preferred_element_type=F32)
        m = (gate_sb * y_sb + gate_cv * y_cv).astype(BF16)
        o_ref[rows, :] = x + jnp.dot(m, wo_ref[...], preferred_element_type=F32)


def _mix(h, g, wgate, a, u, cw, cb, lng, lnb, wao, wco, wo, l, *, tm):
    rows = h.shape[0]
    halo_per_tile = tm // CONV_HALO
    row = lambda width: pl.BlockSpec((tm, width), lambda i: (i, 0))
    prev = pl.BlockSpec((CONV_HALO, CONV_CH), lambda i: (jnp.maximum(i * halo_per_tile - 1, 0), 0))
    shift_rows = tm + F32_SUBLANES * ((CONV_K - 1) // F32_SUBLANES)
    return pl.pallas_call(
        functools.partial(_mix_body, tm=tm),
        grid=(rows // tm,),
        in_specs=[row(D_MODEL), _resident((1, D_MODEL)), _layer(wgate, l), row(SB_WIDTH),
                  row(CONV_CH), prev, _resident(cw.shape), _resident((1, CONV_CH)),
                  _resident((1, CONV_CH)), _resident((1, CONV_CH)), _layer(wao, l),
                  _layer(wco, l), _layer(wo, l)],
        out_specs=row(D_MODEL),
        out_shape=jax.ShapeDtypeStruct((rows, D_MODEL), F32),
        scratch_shapes=[pltpu.VMEM((CONV_HALO + tm, CONV_CH), F32),
                        pltpu.VMEM((F32_SUBLANES, shift_rows // F32_SUBLANES, F32_SUBLANES, CONV_CH), F32),
                        pltpu.VMEM((tm, CONV_CH), BF16)],
        compiler_params=_params(1),
        name="mixer_out",
    )(h, g, wgate, a, u, u, cw, cb, lng, lnb, wao, wco, wo)


def kernel(x, meta, ffn1_norm, ffn1_w_gate, ffn1_w_up, ffn1_w_down, mix_norm, w_in, w_attn_out,
           conv_w, conv_b, conv_ln_g, conv_ln_b, w_conv_out, w_out, ffn2_norm, ffn2_w_gate,
           ffn2_w_up, ffn2_w_down, final_norm):
    batch, seq, d = x.shape
    depth = w_in.shape[0]
    assert d == D_MODEL and meta.shape == (N_META, D_MODEL) and seq % ATT_BLOCK == 0
    assert N_META <= ATT_BLOCK
    pad = ATT_BLOCK - N_META
    lp = seq + ATT_BLOCK
    tm_ffn = 1040
    tm_mix = 640
    tm_out = 1024
    assert lp % tm_ffn == 0 and lp % tm_mix == 0 and tm_mix % CONV_HALO == 0
    assert tm_mix % CONV_CHUNK == 0 and seq % tm_out == 0

    head = jnp.concatenate([jnp.zeros((pad, d), x.dtype), meta.astype(x.dtype)], axis=0)
    h = x.reshape(batch * seq, d)

    vec = lambda p: p.reshape(1, -1).astype(F32)
    conv_taps = lambda w: jnp.broadcast_to(w.astype(F32)[:, None, :], (CONV_K, F32_SUBLANES, CONV_CH))
    ffn1 = [_to_bf16(w) for w in (ffn1_w_gate, ffn1_w_up, ffn1_w_down)]
    ffn2 = [_to_bf16(w) for w in (ffn2_w_gate, ffn2_w_up, ffn2_w_down)]
    w_proj, w_gates = _to_bf16(w_in, (3 * SB_WIDTH + 2 * CONV_CH, w_in.shape[2]))
    w_ao, w_co, w_o = _to_bf16(w_attn_out), _to_bf16(w_conv_out), _to_bf16(w_out)
    for l in range(depth):
        ends = dict(frames=(lp, seq), head=head) if l == 0 else {}
        h = _ffn(h, vec(ffn1_norm[l]), *ffn1, l, tm=tm_ffn, **ends)
        q, k, v, u = _proj(h, vec(mix_norm[l]), w_proj, l, batch=batch, lp=lp, pad=pad, tm=tm_ffn)
        a = _attention(q, k, v, batch=batch, lp=lp, pad=pad)
        h = _mix(h, vec(mix_norm[l]), w_gates, a, u, conv_taps(conv_w[l]), vec(conv_b[l]),
                 vec(conv_ln_g[l]), vec(conv_ln_b[l]), w_ao, w_co, w_o, l, tm=tm_mix)
        if l < depth - 1:
            h = _ffn(h, vec(ffn2_norm[l]), *ffn2, l, tm=tm_ffn)
    out = _ffn(h, vec(ffn2_norm[depth - 1]), *ffn2, depth - 1, vec(final_norm), tm=tm_out,
               frames=(lp, seq))
    return out.reshape(batch, seq, d)
```

```python
import functools
import math

import jax
import jax.numpy as jnp
from jax import lax
from jax.experimental import pallas as pl
from jax.experimental.pallas import tpu as pltpu

D_MODEL = 1024
N_META = 16
SB_HEADS = 8
SB_HEAD_DIM = 64
SB_WIDTH = SB_HEADS * SB_HEAD_DIM
HEAD_PAIR = 2 * SB_HEAD_DIM
CONV_CH = 512
CONV_K = 31
D_FF = 2816
NORM_EPS = 1e-6
LN_EPS = 1e-5
LOG2E = 1.4426950408889634

F32_SUBLANES = 8
BF16_SUBLANES = 16
VREG_LANES = 128
ATT_BLOCK = 128
ATT_HEAD_GROUPS = 2
ATT_QBLOCKS = 5
ATT_FIXED = 3
ATT_TAIL_ROWS = 48
MXU_COLS = 256
FF_CHUNK = MXU_COLS
CONV_HALO = 32
CONV_CHUNK = 32
CAST_STEPS = 2
VMEM_LIMIT = 56 * 1024 * 1024

SKIP_LOG = -104.0
MASKED = 1e30

F32 = jnp.float32
BF16 = jnp.bfloat16


def _rmsnorm(x, g):
    return x * lax.rsqrt(jnp.mean(x * x, axis=-1, keepdims=True) + NORM_EPS) * g


def _sigmoid(x):
    return 1.0 / (1.0 + jnp.exp(-x))


def _resident(shape):
    return pl.BlockSpec(shape, lambda *_: (0,) * len(shape), pipeline_mode=pl.Buffered(1))


def _layer(w, l):
    return pl.BlockSpec((None,) + w.shape[1:], lambda *_: (l, 0, 0), pipeline_mode=pl.Buffered(1))


def _params(n_axes):
    return pltpu.CompilerParams(dimension_semantics=("arbitrary",) * n_axes,
                                vmem_limit_bytes=VMEM_LIMIT)


def _cast_body(w_ref, *o_refs, splits):
    c0 = 0
    for o_ref, c1 in zip(o_refs, splits):
        o_ref[...] = w_ref[:, c0:c1].astype(BF16)
        c0 = c1


def _to_bf16(w, splits=None):
    depth, k, n = w.shape
    splits = tuple(splits or (n,))
    widths = [c1 - c0 for c0, c1 in zip((0,) + splits[:-1], splits)]
    rows = k // CAST_STEPS
    assert splits[-1] == n and k % CAST_STEPS == 0 and rows % (2 * F32_SUBLANES) == 0
    out = pl.pallas_call(
        functools.partial(_cast_body, splits=splits),
        grid=(depth, CAST_STEPS),
        in_specs=[pl.BlockSpec((None, rows, n), lambda l, i: (l, i, 0))],
        out_specs=[pl.BlockSpec((None, rows, wd), lambda l, i: (l, i, 0)) for wd in widths],
        out_shape=[jax.ShapeDtypeStruct((depth, k, wd), BF16) for wd in widths],
        compiler_params=_params(2),
        name="to_bf16",
    )(w)
    return out if len(out) > 1 else out[0]


def _ffn_body(h_ref, g_ref, wg_ref, wu_ref, wd_ref, *rest, final, tiles_per_batch):
    if final:
        fg_ref, o_ref, act_ref = rest
    elif tiles_per_batch:
        head_ref, o_ref, act_ref = rest
    else:
        o_ref, act_ref = rest
    x = h_ref[...]
    if tiles_per_batch and not final:
        n_head = head_ref.shape[0]
        first = jnp.concatenate([head_ref[...], x[:x.shape[0] - n_head]], axis=0)
        x = jnp.where(pl.program_id(0) % tiles_per_batch == 0, first, x)
    xn = _rmsnorm(x, g_ref[...]).astype(BF16)
    for c0 in range(0, D_FF, FF_CHUNK):
        c1 = min(c0 + FF_CHUNK, D_FF)
        gate = jnp.dot(xn, wg_ref[:, c0:c1], preferred_element_type=F32)
        up = jnp.dot(xn, wu_ref[:, c0:c1], preferred_element_type=F32)
        act_ref[:, c0:c1] = (gate * _sigmoid(gate) * up).astype(BF16)
    y = x + 0.5 * jnp.dot(act_ref[...], wd_ref[...], preferred_element_type=F32)
    if final:
        y = _rmsnorm(y, fg_ref[...])
    o_ref[...] = y


def _ffn(h, g, wg, wu, wd, l, final_g=None, *, tm, frames=None, head=None):
    rows = h.shape[0]
    final = final_g is not None
    row_spec = pl.BlockSpec((tm, D_MODEL), lambda i: (i, 0))
    in_spec = row_spec
    tiles_per_batch = None
    if frames is not None:
        lp, seq = frames
        if final:
            tiles_per_batch = seq // tm
            rows = rows // lp * seq
            start = lambda i: i // tiles_per_batch * lp + (lp - seq) + i % tiles_per_batch * tm
        else:
            tiles_per_batch = lp // tm
            rows = rows // seq * lp
            start = lambda i: (i // tiles_per_batch * seq
                               + jnp.maximum(i % tiles_per_batch * tm - (lp - seq), 0))
        in_spec = pl.BlockSpec((pl.Element(tm), pl.Element(D_MODEL)),
                               lambda i: (pl.multiple_of(start(i), F32_SUBLANES), 0))
    in_specs = [in_spec, _resident((1, D_MODEL)), _layer(wg, l), _layer(wu, l), _layer(wd, l)]
    args = [h, g, wg, wu, wd]
    if final:
        in_specs.append(_resident((1, D_MODEL)))
        args.append(final_g)
    elif head is not None:
        in_specs.append(_resident(head.shape))
        args.append(head)
    return pl.pallas_call(
        functools.partial(_ffn_body, final=final, tiles_per_batch=tiles_per_batch),
        grid=(rows // tm,),
        in_specs=in_specs,
        out_specs=row_spec,
        out_shape=jax.ShapeDtypeStruct((rows, D_MODEL), F32),
        scratch_shapes=[pltpu.VMEM((tm, D_FF), BF16)],
        compiler_params=_params(1),
        name="ffn_final" if final else "ffn",
    )(*args)


def _proj_body(h_ref, g_ref, w_ref, q_ref, k_ref, v_ref, u_ref, *, tm, pad):
    xn = _rmsnorm(h_ref[...], g_ref[...]).astype(BF16)

    def col(i):
        return jnp.dot(xn, w_ref[:, i * SB_WIDTH:(i + 1) * SB_WIDTH], preferred_element_type=F32)

    q_ref[...] = (col(0) * (-1.0 / math.sqrt(SB_HEAD_DIM))).astype(BF16)
    k_ref[...] = col(1).astype(BF16)
    v_ref[...] = col(2).astype(BF16)
    u = col(3) * _sigmoid(col(4))
    pos = pl.program_id(1) * tm + lax.broadcasted_iota(jnp.int32, (tm, 1), 0)
    u_ref[...] = jnp.where(pos >= pad, u, 0.0)


def _proj(h, g, w, l, *, batch, lp, pad, tm):
    rows = h.shape[0]
    nt = lp // tm
    row = lambda width: pl.BlockSpec((tm, width), lambda b, j: (b * nt + j, 0))
    bf = jax.ShapeDtypeStruct((rows, SB_WIDTH), BF16)
    return pl.pallas_call(
        functools.partial(_proj_body, tm=tm, pad=pad),
        grid=(batch, nt),
        in_specs=[row(D_MODEL), _resident((1, D_MODEL)), _layer(w, l)],
        out_specs=[row(SB_WIDTH), row(SB_WIDTH), row(SB_WIDTH), row(CONV_CH)],
        out_shape=[bf, bf, bf, jax.ShapeDtypeStruct((rows, CONV_CH), F32)],
        compiler_params=_params(2),
        name="mixer_proj",
    )(h, g, w)


def _attn_body(q_ref, k_ref, v_ref, o_ref, acc_ref, carry_ref, *, pad):
    blk = ATT_BLOCK
    i0 = pl.program_id(1) * ATT_QBLOCKS
    row = lax.broadcasted_iota(jnp.int32, (blk, blk), 0)
    lane = lax.broadcasted_iota(jnp.int32, (blk, blk), 1)
    half = jnp.concatenate([(row > lane).astype(BF16), jnp.ones((blk, blk), BF16)], axis=1)
    suffix_and_total = jnp.concatenate([half, half], axis=0)
    first = (lane < SB_HEAD_DIM).astype(BF16)
    second = (lane >= SB_HEAD_DIM).astype(BF16)
    nt = (((1,), (1,)), ((), ()))

    pair_cols = [slice(m * HEAD_PAIR, (m + 1) * HEAD_PAIR) for m in range(SB_HEADS // 2)]
    head_groups = [range(g, g + SB_HEADS // ATT_HEAD_GROUPS)
                   for g in range(0, SB_HEADS, SB_HEADS // ATT_HEAD_GROUPS)]
    block_rows = lambda t: slice(t * blk, (t + 1) * blk)

    def add_rows(old, new):
        if old is None:
            assert new.shape[0] == blk
            return new
        nr = new.shape[0]
        top = old[:nr] + new
        return top if nr == blk else jnp.concatenate([top, old[nr:]], axis=0)

    def sweep(units, carry, acc):
        keys, masked = {}, []
        for t, key, j, nr in units:
            if key not in keys:
                start = pl.multiple_of(jnp.maximum(j, 0) * blk, blk)
                keys[key] = (start, j, k_ref[pl.ds(start, blk), :], v_ref[pl.ds(start, blk), :])
            start, jj = keys[key][:2]
            valid = (start + lane < (i0 + t) * blk + row) & (start + lane >= pad) & (jj >= 0)
            masked.append(jnp.where(valid, 0.0, MASKED)[:nr])
        visitors = {key: [u for u, unit in enumerate(units) if unit[1] == key] for key in keys}
        q_pair = {t: [(q_ref[block_rows(t), c] * first, q_ref[block_rows(t), c] * second)
                      for c in pair_cols] for t in {unit[0] for unit in units}}

        def per_visitor(key, m, stacked):
            out, r0 = {}, 0
            for u in visitors[key]:
                nr = units[u][3]
                out[u, m] = stacked[r0:r0 + 2 * nr]
                r0 += 2 * nr
            return out

        zneg = {}
        for key, (_, _, kt, _) in keys.items():
            for m, c in enumerate(pair_cols):
                lhs = [q[:units[u][3]] for u in visitors[key] for q in q_pair[units[u][0]][m]]
                zneg.update(per_visitor(key, m, lax.dot_general(
                    jnp.concatenate(lhs, axis=0), kt[:, c], nt, preferred_element_type=F32)))
        log_beta, sums = {}, {}
        for heads in head_groups:
            split = []
            for u, (_, _, _, nr) in enumerate(units):
                for h in heads:
                    zn = zneg[u, h // 2][(h % 2) * nr:(h % 2 + 1) * nr] + masked[u]
                    lk = jnp.minimum(zn, 0.0) - jnp.log(1.0 + jnp.exp2(jnp.abs(zn) * (-LOG2E)))
                    hi = lk.astype(BF16)
                    lo = (lk - hi.astype(F32)).astype(BF16)
                    log_beta[u, h] = lk - zn
                    split.append(jnp.concatenate([hi, lo], axis=1))
            group_sums = jnp.dot(jnp.concatenate(split, axis=0), suffix_and_total,
                                 preferred_element_type=F32)
            r0 = 0
            for u, (_, _, _, nr) in enumerate(units):
                for h in heads:
                    sums[u, h] = group_sums[r0:r0 + nr]
                    r0 += nr

        carry = {t: list(c) for t, c in carry.items()}
        acc = {t: list(a) for t, a in acc.items()}
        probs, carry_after = {}, []
        for u, (t, _, _, nr) in enumerate(units):
            for h in range(SB_HEADS):
                after = sums[u, h][:, :blk]
                if carry[t][h] is not None:
                    after = after + carry[t][h][:nr]
                probs[u, h] = jnp.exp(log_beta[u, h] + after).astype(BF16)
                carry[t][h] = add_rows(carry[t][h], sums[u, h][:, blk:])
            carry_after.append(list(carry[t]))
        outs = {}
        for key, (_, _, _, vt) in keys.items():
            for m, c in enumerate(pair_cols):
                lhs = [probs[u, 2 * m + s] for u in visitors[key] for s in range(2)]
                outs.update(per_visitor(key, m, jnp.dot(jnp.concatenate(lhs, axis=0), vt[:, c],
                                                        preferred_element_type=F32)))
        states = []
        for u, (t, _, _, nr) in enumerate(units):
            own = lax.broadcasted_iota(jnp.int32, (nr, HEAD_PAIR), 1) < SB_HEAD_DIM
            for m in range(SB_HEADS // 2):
                acc[t][m] = add_rows(acc[t][m], jnp.where(own, outs[u, m][:nr], outs[u, m][nr:]))
            states.append((carry_after[u], list(acc[t])))
        return states

    def store_state(t, carry, acc):
        for h in range(SB_HEADS):
            carry_ref[t, h] = carry[h]
        for m, c in enumerate(pair_cols):
            acc_ref[block_rows(t), c] = acc[m]

    def is_live(carry):
        return (jnp.max(functools.reduce(jnp.maximum, carry)) > SKIP_LOG).astype(jnp.int32)

    query_blocks = range(ATT_QBLOCKS)
    fixed = [(t, t - d, i0 + t - d, blk if d < ATT_FIXED - 1 else ATT_TAIL_ROWS)
             for t in query_blocks for d in range(ATT_FIXED)]
    states = sweep(fixed, {t: [None] * SB_HEADS for t in query_blocks},
                   {t: [None] * (SB_HEADS // 2) for t in query_blocks})
    live = []
    for t in query_blocks:
        store_state(t, *states[ATT_FIXED * t + ATT_FIXED - 2])
        carry, acc = states[ATT_FIXED * t + ATT_FIXED - 1]
        live.append(is_live(carry))
        o_ref[block_rows(t), :] = jnp.concatenate(acc, axis=1).astype(BF16)

    for t in query_blocks:
        def body(state, t=t):
            j, _ = state
            carry, acc = sweep([(t, 0, j, blk)], {t: [carry_ref[t, h] for h in range(SB_HEADS)]},
                               {t: [acc_ref[block_rows(t), c] for c in pair_cols]})[-1]
            store_state(t, carry, acc)
            return j - 1, is_live(carry)

        lax.while_loop(lambda s: (s[0] >= 0) & (s[1] > 0), body,
                       (i0 + t - (ATT_FIXED - 1), live[t]))

        @pl.when(live[t] > 0)
        def _(t=t):
            o_ref[block_rows(t), :] = acc_ref[block_rows(t), :].astype(BF16)


def _attention(q, k, v, *, batch, lp, pad):
    steps = lp // (ATT_BLOCK * ATT_QBLOCKS)
    assert lp == steps * ATT_BLOCK * ATT_QBLOCKS
    qspec = pl.BlockSpec((ATT_QBLOCKS * ATT_BLOCK, SB_WIDTH), lambda b, g: (b * steps + g, 0))
    kvspec = pl.BlockSpec((lp, SB_WIDTH), lambda b, g: (b, 0))
    return pl.pallas_call(
        functools.partial(_attn_body, pad=pad),
        grid=(batch, steps),
        in_specs=[qspec, kvspec, kvspec],
        out_specs=qspec,
        out_shape=jax.ShapeDtypeStruct(q.shape, BF16),
        scratch_shapes=[pltpu.VMEM((ATT_QBLOCKS * ATT_BLOCK, SB_WIDTH), F32),
                        pltpu.VMEM((ATT_QBLOCKS, SB_HEADS, ATT_BLOCK, ATT_BLOCK), F32)],
        compiler_params=_params(2),
        name="sb_attention",
    )(q, k, v)


def _mix_body(h_ref, g_ref, wgate_ref, a_ref, u_ref, uprev_ref, cw_ref, cb_ref, lng_ref, lnb_ref,
              wao_ref, wco_ref, wo_ref, o_ref, ubuf_ref, shift_ref, conv_ref, *, tm):
    ubuf_ref[0:CONV_HALO, :] = uprev_ref[...]
    ubuf_ref[CONV_HALO:, :] = u_ref[...]
    base = CONV_HALO - (CONV_K - 1)
    sub = F32_SUBLANES
    for r in range(sub):
        n = tm + sub * ((CONV_K - 1 - r) // sub)
        shift_ref[r, 0:n // sub] = ubuf_ref[base + r:base + r + n, :].reshape(n // sub, sub, CONV_CH)

    def shift_zero(value, rows):
        bits = lax.bitcast_convert_type(value[:rows, :VREG_LANES], jnp.uint32)
        bits = lax.shift_right_logical(lax.shift_right_logical(bits, jnp.uint32(16)), jnp.uint32(16))
        return lax.bitcast_convert_type(bits, F32)

    def conv_chunk(c0, after=None):
        y = jnp.broadcast_to(cb_ref[...], (CONV_CHUNK // sub, sub, CONV_CH))
        if after is not None:
            first = jnp.concatenate([y[0, :, :VREG_LANES] + shift_zero(after, sub),
                                     y[0, :, VREG_LANES:]], axis=1)
            y = jnp.concatenate([first[None], y[1:]], axis=0)
        for k in range(CONV_K):
            g0 = (c0 + k - k % sub) // sub
            y = y + cw_ref[k] * shift_ref[k % sub, g0:g0 + CONV_CHUNK // sub]
        y = y.reshape(CONV_CHUNK, CONV_CH)
        mu = jnp.mean(y, axis=-1, keepdims=True)
        yc = y - mu
        var = jnp.mean(yc * yc, axis=-1, keepdims=True)
        ln = yc * lax.rsqrt(var + LN_EPS) * lng_ref[...] + lnb_ref[...]
        act = ln * _sigmoid(ln)
        conv_ref[c0:c0 + CONV_CHUNK, :] = act.astype(BF16)
        return act

    def pinned_after(lhs, value):
        if value is None:
            return lhs
        zero = shift_zero(value, BF16_SUBLANES).astype(BF16)
        top = jnp.concatenate([lhs[:BF16_SUBLANES, :VREG_LANES] + zero,
                               lhs[:BF16_SUBLANES, VREG_LANES:]], axis=1)
        return jnp.concatenate([top, lhs[BF16_SUBLANES:]], axis=0)

    x = h_ref[...]
    xn = _rmsnorm(x, g_ref[...]).astype(BF16)
    cols = 2 * MXU_COLS
    blocks = ([(xn, wgate_ref, c) for c in range(0, 2 * D_MODEL, cols)]
              + [(a_ref[...], wao_ref, c) for c in range(0, D_MODEL, cols)])
    chunks = list(range(0, tm, CONV_CHUNK))
    results, last, done = [], None, 0
    for n, (lhs, w_ref, c) in enumerate(blocks):
        results.append(jnp.dot(pinned_after(lhs, last), w_ref[:, c:c + cols],
                               preferred_element_type=F32))
        upto = (n + 1) * len(chunks) // len(blocks)
        for c0 in chunks[done:upto]:
            last = conv_chunk(c0, results[n - 1] if n else None)
        done = upto
    n_gate = D_MODEL // cols
    gate_sb = _sigmoid(jnp.concatenate(results[:n_gate], axis=1))
    gate_cv = _sigmoid(jnp.concatenate(results[n_gate:2 * n_gate], axis=1))
    y_sb = jnp.concatenate(results[2 * n_gate:], axis=1)
    y_cv = jnp.dot(conv_ref[...], wco_ref[...], preferred_element_type=F32)
    m = (gate_sb * y_sb + gate_cv * y_cv).astype(BF16)
    o_ref[...] = x + jnp.dot(m, wo_ref[...], preferred_element_type=F32)


def _mix(h, g, wgate, a, u, cw, cb, lng, lnb, wao, wco, wo, l, *, tm):
    rows = h.shape[0]
    halo_per_tile = tm // CONV_HALO
    row = lambda width: pl.BlockSpec((tm, width), lambda i: (i, 0))
    prev = pl.BlockSpec((CONV_HALO, CONV_CH), lambda i: (jnp.maximum(i * halo_per_tile - 1, 0), 0))
    shift_rows = tm + F32_SUBLANES * ((CONV_K - 1) // F32_SUBLANES)
    return pl.pallas_call(
        functools.partial(_mix_body, tm=tm),
        grid=(rows // tm,),
        in_specs=[row(D_MODEL), _resident((1, D_MODEL)), _layer(wgate, l), row(SB_WIDTH),
                  row(CONV_CH), prev, _resident(cw.shape), _resident((1, CONV_CH)),
                  _resident((1, CONV_CH)), _resident((1, CONV_CH)), _layer(wao, l),
                  _layer(wco, l), _layer(wo, l)],
        out_specs=row(D_MODEL),
        out_shape=jax.ShapeDtypeStruct((rows, D_MODEL), F32),
        scratch_shapes=[pltpu.VMEM((CONV_HALO + tm, CONV_CH), F32),
                        pltpu.VMEM((F32_SUBLANES, shift_rows // F32_SUBLANES, F32_SUBLANES, CONV_CH), F32),
                        pltpu.VMEM((tm, CONV_CH), BF16)],
        compiler_params=_params(1),
        name="mixer_out",
    )(h, g, wgate, a, u, u, cw, cb, lng, lnb, wao, wco, wo)


def kernel(x, meta, ffn1_norm, ffn1_w_gate, ffn1_w_up, ffn1_w_down, mix_norm, w_in, w_attn_out,
           conv_w, conv_b, conv_ln_g, conv_ln_b, w_conv_out, w_out, ffn2_norm, ffn2_w_gate,
           ffn2_w_up, ffn2_w_down, final_norm):
    batch, seq, d = x.shape
    depth = w_in.shape[0]
    assert d == D_MODEL and meta.shape == (N_META, D_MODEL) and seq % ATT_BLOCK == 0
    assert N_META <= ATT_BLOCK
    pad = ATT_BLOCK - N_META
    lp = seq + ATT_BLOCK
    tm_ffn = 1040
    tm_mix = 640
    tm_out = 1024
    assert lp % tm_ffn == 0 and lp % tm_mix == 0 and tm_mix % CONV_HALO == 0
    assert tm_mix % CONV_CHUNK == 0 and seq % tm_out == 0

    head = jnp.concatenate([jnp.zeros((pad, d), x.dtype), meta.astype(x.dtype)], axis=0)
    h = x.reshape(batch * seq, d)

    vec = lambda p: p.reshape(1, -1).astype(F32)
    conv_taps = lambda w: jnp.broadcast_to(w.astype(F32)[:, None, :], (CONV_K, F32_SUBLANES, CONV_CH))
    ffn1 = [_to_bf16(w) for w in (ffn1_w_gate, ffn1_w_up, ffn1_w_down)]
    ffn2 = [_to_bf16(w) for w in (ffn2_w_gate, ffn2_w_up, ffn2_w_down)]
    w_proj, w_gates = _to_bf16(w_in, (3 * SB_WIDTH + 2 * CONV_CH, w_in.shape[2]))
    w_ao, w_co, w_o = _to_bf16(w_attn_out), _to_bf16(w_conv_out), _to_bf16(w_out)
    for l in range(depth):
        ends = dict(frames=(lp, seq), head=head) if l == 0 else {}
        h = _ffn(h, vec(ffn1_norm[l]), *ffn1, l, tm=tm_ffn, **ends)
        q, k, v, u = _proj(h, vec(mix_norm[l]), w_proj, l, batch=batch, lp=lp, pad=pad, tm=tm_ffn)
        a = _attention(q, k, v, batch=batch, lp=lp, pad=pad)
        h = _mix(h, vec(mix_norm[l]), w_gates, a, u, conv_taps(conv_w[l]), vec(conv_b[l]),
                 vec(conv_ln_g[l]), vec(conv_ln_b[l]), w_ao, w_co, w_o, l, tm=tm_mix)
        if l < depth - 1:
            h = _ffn(h, vec(ffn2_norm[l]), *ffn2, l, tm=tm_ffn)
    out = _ffn(h, vec(ffn2_norm[depth - 1]), *ffn2, depth - 1, vec(final_norm), tm=tm_out,
               frames=(lp, seq))
    return out.reshape(batch, seq, d)
```

```python
import functools
import math

import jax
import jax.numpy as jnp
from jax import lax
from jax.experimental import pallas as pl
from jax.experimental.pallas import tpu as pltpu

D_MODEL = 1024
N_META = 16
SB_HEADS = 8
SB_HEAD_DIM = 64
SB_WIDTH = SB_HEADS * SB_HEAD_DIM
HEAD_PAIR = 2 * SB_HEAD_DIM
CONV_CH = 512
CONV_K = 31
D_FF = 2816
NORM_EPS = 1e-6
LN_EPS = 1e-5
LOG2E = 1.4426950408889634

F32_SUBLANES = 8
ATT_BLOCK = 128
ATT_HEAD_GROUPS = 2
ATT_QBLOCKS = 5
ATT_FIXED = 3
ATT_TAIL_ROWS = 48
MXU_COLS = 256
FF_CHUNK = MXU_COLS
CONV_HALO = 32
CONV_CHUNK = 32
MIX_ROWS = 320
CAST_STEPS = 2
VMEM_LIMIT = 56 * 1024 * 1024

SKIP_LOG = -104.0
MASKED = 1e30

F32 = jnp.float32
BF16 = jnp.bfloat16


def _rmsnorm(x, g):
    return x * lax.rsqrt(jnp.mean(x * x, axis=-1, keepdims=True) + NORM_EPS) * g


def _sigmoid(x):
    return 1.0 / (1.0 + jnp.exp(-x))


def _resident(shape):
    return pl.BlockSpec(shape, lambda *_: (0,) * len(shape), pipeline_mode=pl.Buffered(1))


def _layer(w, l):
    return pl.BlockSpec((None,) + w.shape[1:], lambda *_: (l, 0, 0), pipeline_mode=pl.Buffered(1))


def _params(n_axes):
    return pltpu.CompilerParams(dimension_semantics=("arbitrary",) * n_axes,
                                vmem_limit_bytes=VMEM_LIMIT)


def _cast_body(w_ref, *o_refs, splits):
    c0 = 0
    for o_ref, c1 in zip(o_refs, splits):
        o_ref[...] = w_ref[:, c0:c1].astype(BF16)
        c0 = c1


def _to_bf16(w, splits=None):
    depth, k, n = w.shape
    splits = tuple(splits or (n,))
    widths = [c1 - c0 for c0, c1 in zip((0,) + splits[:-1], splits)]
    rows = k // CAST_STEPS
    assert splits[-1] == n and k % CAST_STEPS == 0 and rows % (2 * F32_SUBLANES) == 0
    out = pl.pallas_call(
        functools.partial(_cast_body, splits=splits),
        grid=(depth, CAST_STEPS),
        in_specs=[pl.BlockSpec((None, rows, n), lambda l, i: (l, i, 0))],
        out_specs=[pl.BlockSpec((None, rows, wd), lambda l, i: (l, i, 0)) for wd in widths],
        out_shape=[jax.ShapeDtypeStruct((depth, k, wd), BF16) for wd in widths],
        compiler_params=_params(2),
        name="to_bf16",
    )(w)
    return out if len(out) > 1 else out[0]


def _ffn_body(h_ref, g_ref, wg_ref, wu_ref, wd_ref, *rest, final, tiles_per_batch):
    if final:
        fg_ref, o_ref, act_ref = rest
    elif tiles_per_batch:
        head_ref, o_ref, act_ref = rest
    else:
        o_ref, act_ref = rest
    x = h_ref[...]
    if tiles_per_batch and not final:
        n_head = head_ref.shape[0]
        first = jnp.concatenate([head_ref[...], x[:x.shape[0] - n_head]], axis=0)
        x = jnp.where(pl.program_id(0) % tiles_per_batch == 0, first, x)
    xn = _rmsnorm(x, g_ref[...]).astype(BF16)
    for c0 in range(0, D_FF, FF_CHUNK):
        c1 = min(c0 + FF_CHUNK, D_FF)
        gate = jnp.dot(xn, wg_ref[:, c0:c1], preferred_element_type=F32)
        up = jnp.dot(xn, wu_ref[:, c0:c1], preferred_element_type=F32)
        act_ref[:, c0:c1] = (gate * _sigmoid(gate) * up).astype(BF16)
    y = x + 0.5 * jnp.dot(act_ref[...], wd_ref[...], preferred_element_type=F32)
    if final:
        y = _rmsnorm(y, fg_ref[...])
    o_ref[...] = y


def _ffn(h, g, wg, wu, wd, l, final_g=None, *, tm, frames=None, head=None):
    rows = h.shape[0]
    final = final_g is not None
    row_spec = pl.BlockSpec((tm, D_MODEL), lambda i: (i, 0))
    in_spec = row_spec
    tiles_per_batch = None
    if frames is not None:
        lp, seq = frames
        if final:
            tiles_per_batch = seq // tm
            rows = rows // lp * seq
            start = lambda i: i // tiles_per_batch * lp + (lp - seq) + i % tiles_per_batch * tm
        else:
            tiles_per_batch = lp // tm
            rows = rows // seq * lp
            start = lambda i: (i // tiles_per_batch * seq
                               + jnp.maximum(i % tiles_per_batch * tm - (lp - seq), 0))
        in_spec = pl.BlockSpec((pl.Element(tm), pl.Element(D_MODEL)),
                               lambda i: (pl.multiple_of(start(i), F32_SUBLANES), 0))
    in_specs = [in_spec, _resident((1, D_MODEL)), _layer(wg, l), _layer(wu, l), _layer(wd, l)]
    args = [h, g, wg, wu, wd]
    if final:
        in_specs.append(_resident((1, D_MODEL)))
        args.append(final_g)
    elif head is not None:
        in_specs.append(_resident(head.shape))
        args.append(head)
    return pl.pallas_call(
        functools.partial(_ffn_body, final=final, tiles_per_batch=tiles_per_batch),
        grid=(rows // tm,),
        in_specs=in_specs,
        out_specs=row_spec,
        out_shape=jax.ShapeDtypeStruct((rows, D_MODEL), F32),
        scratch_shapes=[pltpu.VMEM((tm, D_FF), BF16)],
        compiler_params=_params(1),
        name="ffn_final" if final else "ffn",
    )(*args)


def _proj_body(h_ref, g_ref, w_ref, q_ref, k_ref, v_ref, u_ref, *, tm, pad):
    xn = _rmsnorm(h_ref[...], g_ref[...]).astype(BF16)

    def col(i):
        return jnp.dot(xn, w_ref[:, i * SB_WIDTH:(i + 1) * SB_WIDTH], preferred_element_type=F32)

    q_ref[...] = (col(0) * (-1.0 / math.sqrt(SB_HEAD_DIM))).astype(BF16)
    k_ref[...] = col(1).astype(BF16)
    v_ref[...] = col(2).astype(BF16)
    u = col(3) * _sigmoid(col(4))
    pos = pl.program_id(1) * tm + lax.broadcasted_iota(jnp.int32, (tm, 1), 0)
    u_ref[...] = jnp.where(pos >= pad, u, 0.0)


def _proj(h, g, w, l, *, batch, lp, pad, tm):
    rows = h.shape[0]
    nt = lp // tm
    row = lambda width: pl.BlockSpec((tm, width), lambda b, j: (b * nt + j, 0))
    bf = jax.ShapeDtypeStruct((rows, SB_WIDTH), BF16)
    return pl.pallas_call(
        functools.partial(_proj_body, tm=tm, pad=pad),
        grid=(batch, nt),
        in_specs=[row(D_MODEL), _resident((1, D_MODEL)), _layer(w, l)],
        out_specs=[row(SB_WIDTH), row(SB_WIDTH), row(SB_WIDTH), row(CONV_CH)],
        out_shape=[bf, bf, bf, jax.ShapeDtypeStruct((rows, CONV_CH), F32)],
        compiler_params=_params(2),
        name="mixer_proj",
    )(h, g, w)


def _attn_body(q_ref, k_ref, v_ref, o_ref, acc_ref, carry_ref, *, pad):
    blk = ATT_BLOCK
    i0 = pl.program_id(1) * ATT_QBLOCKS
    row = lax.broadcasted_iota(jnp.int32, (blk, blk), 0)
    lane = lax.broadcasted_iota(jnp.int32, (blk, blk), 1)
    half = jnp.concatenate([(row > lane).astype(BF16), jnp.ones((blk, blk), BF16)], axis=1)
    suffix_and_total = jnp.concatenate([half, half], axis=0)
    first = (lane < SB_HEAD_DIM).astype(BF16)
    second = (lane >= SB_HEAD_DIM).astype(BF16)
    nt = (((1,), (1,)), ((), ()))

    pair_cols = [slice(m * HEAD_PAIR, (m + 1) * HEAD_PAIR) for m in range(SB_HEADS // 2)]
    head_groups = [range(g, g + SB_HEADS // ATT_HEAD_GROUPS)
                   for g in range(0, SB_HEADS, SB_HEADS // ATT_HEAD_GROUPS)]
    block_rows = lambda t: slice(t * blk, (t + 1) * blk)

    def add_rows(old, new):
        if old is None:
            assert new.shape[0] == blk
            return new
        nr = new.shape[0]
        top = old[:nr] + new
        return top if nr == blk else jnp.concatenate([top, old[nr:]], axis=0)

    def sweep(units, carry, acc):
        keys, masked = {}, []
        for t, key, j, nr in units:
            if key not in keys:
                start = pl.multiple_of(jnp.maximum(j, 0) * blk, blk)
                keys[key] = (start, j, k_ref[pl.ds(start, blk), :], v_ref[pl.ds(start, blk), :])
            start, jj = keys[key][:2]
            valid = (start + lane < (i0 + t) * blk + row) & (start + lane >= pad) & (jj >= 0)
            masked.append(jnp.where(valid, 0.0, MASKED)[:nr])
        visitors = {key: [u for u, unit in enumerate(units) if unit[1] == key] for key in keys}
        q_pair = {t: [(q_ref[block_rows(t), c] * first, q_ref[block_rows(t), c] * second)
                      for c in pair_cols] for t in {unit[0] for unit in units}}

        def per_visitor(key, m, stacked):
            out, r0 = {}, 0
            for u in visitors[key]:
                nr = units[u][3]
                out[u, m] = stacked[r0:r0 + 2 * nr]
                r0 += 2 * nr
            return out

        zneg = {}
        for key, (_, _, kt, _) in keys.items():
            for m, c in enumerate(pair_cols):
                lhs = [q[:units[u][3]] for u in visitors[key] for q in q_pair[units[u][0]][m]]
                zneg.update(per_visitor(key, m, lax.dot_general(
                    jnp.concatenate(lhs, axis=0), kt[:, c], nt, preferred_element_type=F32)))
        log_beta, sums = {}, {}
        for heads in head_groups:
            split = []
            for u, (_, _, _, nr) in enumerate(units):
                for h in heads:
                    zn = zneg[u, h // 2][(h % 2) * nr:(h % 2 + 1) * nr] + masked[u]
                    lk = jnp.minimum(zn, 0.0) - jnp.log(1.0 + jnp.exp2(jnp.abs(zn) * (-LOG2E)))
                    hi = lk.astype(BF16)
                    lo = (lk - hi.astype(F32)).astype(BF16)
                    log_beta[u, h] = lk - zn
                    split.append(jnp.concatenate([hi, lo], axis=1))
            group_sums = jnp.dot(jnp.concatenate(split, axis=0), suffix_and_total,
                                 preferred_element_type=F32)
            r0 = 0
            for u, (_, _, _, nr) in enumerate(units):
                for h in heads:
                    sums[u, h] = group_sums[r0:r0 + nr]
                    r0 += nr

        carry = {t: list(c) for t, c in carry.items()}
        acc = {t: list(a) for t, a in acc.items()}
        probs, carry_after = {}, []
        for u, (t, _, _, nr) in enumerate(units):
            for h in range(SB_HEADS):
                after = sums[u, h][:, :blk]
                if carry[t][h] is not None:
                    after = after + carry[t][h][:nr]
                probs[u, h] = jnp.exp(log_beta[u, h] + after).astype(BF16)
                carry[t][h] = add_rows(carry[t][h], sums[u, h][:, blk:])
            carry_after.append(list(carry[t]))
        outs = {}
        for key, (_, _, _, vt) in keys.items():
            for m, c in enumerate(pair_cols):
                lhs = [probs[u, 2 * m + s] for u in visitors[key] for s in range(2)]
                outs.update(per_visitor(key, m, jnp.dot(jnp.concatenate(lhs, axis=0), vt[:, c],
                                                        preferred_element_type=F32)))
        states = []
        for u, (t, _, _, nr) in enumerate(units):
            own = lax.broadcasted_iota(jnp.int32, (nr, HEAD_PAIR), 1) < SB_HEAD_DIM
            for m in range(SB_HEADS // 2):
                acc[t][m] = add_rows(acc[t][m], jnp.where(own, outs[u, m][:nr], outs[u, m][nr:]))
            states.append((carry_after[u], list(acc[t])))
        return states

    def store_state(t, carry, acc):
        for h in range(SB_HEADS):
            carry_ref[t, h] = carry[h]
        for m, c in enumerate(pair_cols):
            acc_ref[block_rows(t), c] = acc[m]

    def is_live(carry):
        return (jnp.max(functools.reduce(jnp.maximum, carry)) > SKIP_LOG).astype(jnp.int32)

    query_blocks = range(ATT_QBLOCKS)
    fixed = [(t, t - d, i0 + t - d, blk if d < ATT_FIXED - 1 else ATT_TAIL_ROWS)
             for t in query_blocks for d in range(ATT_FIXED)]
    states = sweep(fixed, {t: [None] * SB_HEADS for t in query_blocks},
                   {t: [None] * (SB_HEADS // 2) for t in query_blocks})
    final_carry = {}
    for t in query_blocks:
        store_state(t, *states[ATT_FIXED * t + ATT_FIXED - 2])
        final_carry[t], acc = states[ATT_FIXED * t + ATT_FIXED - 1]
        o_ref[block_rows(t), :] = jnp.concatenate(acc, axis=1).astype(BF16)

    @pl.when(is_live([c for t in query_blocks for c in final_carry[t]]) > 0)
    def _():
        for t in query_blocks:
            live = is_live(final_carry[t])

            def body(state, t=t):
                j, _ = state
                carry, acc = sweep([(t, 0, j, blk)], {t: [carry_ref[t, h] for h in range(SB_HEADS)]},
                                   {t: [acc_ref[block_rows(t), c] for c in pair_cols]})[-1]
                store_state(t, carry, acc)
                return j - 1, is_live(carry)

            lax.while_loop(lambda s: (s[0] >= 0) & (s[1] > 0), body,
                           (i0 + t - (ATT_FIXED - 1), live))

            @pl.when(live > 0)
            def _(t=t):
                o_ref[block_rows(t), :] = acc_ref[block_rows(t), :].astype(BF16)


def _attention(q, k, v, *, batch, lp, pad):
    steps = lp // (ATT_BLOCK * ATT_QBLOCKS)
    assert lp == steps * ATT_BLOCK * ATT_QBLOCKS
    qspec = pl.BlockSpec((ATT_QBLOCKS * ATT_BLOCK, SB_WIDTH), lambda b, g: (b * steps + g, 0))
    kvspec = pl.BlockSpec((lp, SB_WIDTH), lambda b, g: (b, 0))
    return pl.pallas_call(
        functools.partial(_attn_body, pad=pad),
        grid=(batch, steps),
        in_specs=[qspec, kvspec, kvspec],
        out_specs=qspec,
        out_shape=jax.ShapeDtypeStruct(q.shape, BF16),
        scratch_shapes=[pltpu.VMEM((ATT_QBLOCKS * ATT_BLOCK, SB_WIDTH), F32),
                        pltpu.VMEM((ATT_QBLOCKS, SB_HEADS, ATT_BLOCK, ATT_BLOCK), F32)],
        compiler_params=_params(2),
        name="sb_attention",
    )(q, k, v)


def _mix_body(h_ref, g_ref, wgate_ref, a_ref, u_ref, uprev_ref, cw_ref, cb_ref, lng_ref, lnb_ref,
              wao_ref, wco_ref, wo_ref, o_ref, ubuf_ref, shift_ref, conv_ref, *, tm):
    ubuf_ref[0:CONV_HALO, :] = uprev_ref[...]
    ubuf_ref[CONV_HALO:, :] = u_ref[...]
    base = CONV_HALO - (CONV_K - 1)
    sub = F32_SUBLANES
    for r in range(sub):
        n = tm + sub * ((CONV_K - 1 - r) // sub)
        shift_ref[r, 0:n // sub] = ubuf_ref[base + r:base + r + n, :].reshape(n // sub, sub, CONV_CH)

    def conv_chunk(c0):
        y = jnp.broadcast_to(cb_ref[...], (CONV_CHUNK // sub, sub, CONV_CH))
        for k in range(CONV_K):
            g0 = (c0 + k - k % sub) // sub
            y = y + cw_ref[k] * shift_ref[k % sub, g0:g0 + CONV_CHUNK // sub]
        y = y.reshape(CONV_CHUNK, CONV_CH)
        mu = jnp.mean(y, axis=-1, keepdims=True)
        yc = y - mu
        var = jnp.mean(yc * yc, axis=-1, keepdims=True)
        ln = yc * lax.rsqrt(var + LN_EPS) * lng_ref[...] + lnb_ref[...]
        conv_ref[c0:c0 + CONV_CHUNK, :] = (ln * _sigmoid(ln)).astype(BF16)

    for r0 in range(0, tm, MIX_ROWS):
        rows = slice(r0, r0 + MIX_ROWS)
        for c0 in range(r0, r0 + MIX_ROWS, CONV_CHUNK):
            conv_chunk(c0)
        x = h_ref[rows, :]
        xn = _rmsnorm(x, g_ref[...]).astype(BF16)
        gate_sb = _sigmoid(jnp.dot(xn, wgate_ref[:, :D_MODEL], preferred_element_type=F32))
        gate_cv = _sigmoid(jnp.dot(xn, wgate_ref[:, D_MODEL:], preferred_element_type=F32))
        y_sb = jnp.dot(a_ref[rows, :], wao_ref[...], preferred_element_type=F32)
        y_cv = jnp.dot(conv_ref[rows, :], wco_ref[...], preferred_element_type=F32)
        m = (gate_sb * y_sb + gate_cv * y_cv).astype(BF16)
        o_ref[rows, :] = x + jnp.dot(m, wo_ref[...], preferred_element_type=F32)


def _mix(h, g, wgate, a, u, cw, cb, lng, lnb, wao, wco, wo, l, *, tm):
    rows = h.shape[0]
    halo_per_tile = tm // CONV_HALO
    row = lambda width: pl.BlockSpec((tm, width), lambda i: (i, 0))
    prev = pl.BlockSpec((CONV_HALO, CONV_CH), lambda i: (jnp.maximum(i * halo_per_tile - 1, 0), 0))
    shift_rows = tm + F32_SUBLANES * ((CONV_K - 1) // F32_SUBLANES)
    return pl.pallas_call(
        functools.partial(_mix_body, tm=tm),
        grid=(rows // tm,),
        in_specs=[row(D_MODEL), _resident((1, D_MODEL)), _layer(wgate, l), row(SB_WIDTH),
                  row(CONV_CH), prev, _resident(cw.shape), _resident((1, CONV_CH)),
                  _resident((1, CONV_CH)), _resident((1, CONV_CH)), _layer(wao, l),
                  _layer(wco, l), _layer(wo, l)],
        out_specs=row(D_MODEL),
        out_shape=jax.ShapeDtypeStruct((rows, D_MODEL), F32),
        scratch_shapes=[pltpu.VMEM((CONV_HALO + tm, CONV_CH), F32),
                        pltpu.VMEM((F32_SUBLANES, shift_rows // F32_SUBLANES, F32_SUBLANES, CONV_CH), F32),
                        pltpu.VMEM((tm, CONV_CH), BF16)],
        compiler_params=_params(1),
        name="mixer_out",
    )(h, g, wgate, a, u, u, cw, cb, lng, lnb, wao, wco, wo)


def kernel(x, meta, ffn1_norm, ffn1_w_gate, ffn1_w_up, ffn1_w_down, mix_norm, w_in, w_attn_out,
           conv_w, conv_b, conv_ln_g, conv_ln_b, w_conv_out, w_out, ffn2_norm, ffn2_w_gate,
           ffn2_w_up, ffn2_w_down, final_norm):
    batch, seq, d = x.shape
    depth = w_in.shape[0]
    assert d == D_MODEL and meta.shape == (N_META, D_MODEL) and seq % ATT_BLOCK == 0
    assert N_META <= ATT_BLOCK
    pad = ATT_BLOCK - N_META
    lp = seq + ATT_BLOCK
    tm_ffn = 1040
    tm_mix = 640
    tm_out = 1024
    assert lp % tm_ffn == 0 and lp % tm_mix == 0 and tm_mix % CONV_HALO == 0
    assert tm_mix % CONV_CHUNK == 0 and seq % tm_out == 0

    head = jnp.concatenate([jnp.zeros((pad, d), x.dtype), meta.astype(x.dtype)], axis=0)
    h = x.reshape(batch * seq, d)

    vec = lambda p: p.reshape(1, -1).astype(F32)
    conv_taps = lambda w: jnp.broadcast_to(w.astype(F32)[:, None, :], (CONV_K, F32_SUBLANES, CONV_CH))
    ffn1 = [_to_bf16(w) for w in (ffn1_w_gate, ffn1_w_up, ffn1_w_down)]
    ffn2 = [_to_bf16(w) for w in (ffn2_w_gate, ffn2_w_up, ffn2_w_down)]
    w_proj, w_gates = _to_bf16(w_in, (3 * SB_WIDTH + 2 * CONV_CH, w_in.shape[2]))
    w_ao, w_co, w_o = _to_bf16(w_attn_out), _to_bf16(w_conv_out), _to_bf16(w_out)
    for l in range(depth):
        ends = dict(frames=(lp, seq), head=head) if l == 0 else {}
        h = _ffn(h, vec(ffn1_norm[l]), *ffn1, l, tm=tm_ffn, **ends)
        q, k, v, u = _proj(h, vec(mix_norm[l]), w_proj, l, batch=batch, lp=lp, pad=pad, tm=tm_ffn)
        a = _attention(q, k, v, batch=batch, lp=lp, pad=pad)
        h = _mix(h, vec(mix_norm[l]), w_gates, a, u, conv_taps(conv_w[l]), vec(conv_b[l]),
                 vec(conv_ln_g[l]), vec(conv_ln_b[l]), w_ao, w_co, w_o, l, tm=tm_mix)
        if l < depth - 1:
            h = _ffn(h, vec(ffn2_norm[l]), *ffn2, l, tm=tm_ffn)
    out = _ffn(h, vec(ffn2_norm[depth - 1]), *ffn2, depth - 1, vec(final_norm), tm=tm_out,
               frames=(lp, seq))
    return out.reshape(batch, seq, d)
```

```python
import functools
import math

import jax
import jax.numpy as jnp
from jax import lax
from jax.experimental import pallas as pl
from jax.experimental.pallas import tpu as pltpu

D_MODEL = 1024
N_META = 16
SB_HEADS = 8
SB_HEAD_DIM = 64
SB_WIDTH = SB_HEADS * SB_HEAD_DIM
HEAD_PAIR = 2 * SB_HEAD_DIM
CONV_CH = 512
CONV_K = 31
D_FF = 2816
NORM_EPS = 1e-6
LN_EPS = 1e-5
LOG2E = 1.4426950408889634

F32_SUBLANES = 8
ATT_BLOCK = 128
ATT_HEAD_GROUPS = 2
ATT_QBLOCKS = 5
ATT_FIXED = 3
ATT_TAIL_ROWS = 48
MXU_COLS = 256
FF_CHUNK = MXU_COLS
CONV_HALO = 32
CONV_CHUNK = 32
MIX_ROWS = 160
CAST_STEPS = 2
VMEM_LIMIT = 56 * 1024 * 1024

SKIP_LOG = -104.0
MASKED = 1e30

F32 = jnp.float32
BF16 = jnp.bfloat16


def _rmsnorm(x, g):
    return x * lax.rsqrt(jnp.mean(x * x, axis=-1, keepdims=True) + NORM_EPS) * g


def _sigmoid(x):
    return 1.0 / (1.0 + jnp.exp(-x))


def _resident(shape):
    return pl.BlockSpec(shape, lambda *_: (0,) * len(shape), pipeline_mode=pl.Buffered(1))


def _layer(w, l):
    return pl.BlockSpec((None,) + w.shape[1:], lambda *_: (l, 0, 0), pipeline_mode=pl.Buffered(1))


def _params(n_axes):
    return pltpu.CompilerParams(dimension_semantics=("arbitrary",) * n_axes,
                                vmem_limit_bytes=VMEM_LIMIT)


def _cast_body(w_ref, *o_refs, splits):
    c0 = 0
    for o_ref, c1 in zip(o_refs, splits):
        o_ref[...] = w_ref[:, c0:c1].astype(BF16)
        c0 = c1


def _to_bf16(w, splits=None):
    depth, k, n = w.shape
    splits = tuple(splits or (n,))
    widths = [c1 - c0 for c0, c1 in zip((0,) + splits[:-1], splits)]
    rows = k // CAST_STEPS
    assert splits[-1] == n and k % CAST_STEPS == 0 and rows % (2 * F32_SUBLANES) == 0
    out = pl.pallas_call(
        functools.partial(_cast_body, splits=splits),
        grid=(depth, CAST_STEPS),
        in_specs=[pl.BlockSpec((None, rows, n), lambda l, i: (l, i, 0))],
        out_specs=[pl.BlockSpec((None, rows, wd), lambda l, i: (l, i, 0)) for wd in widths],
        out_shape=[jax.ShapeDtypeStruct((depth, k, wd), BF16) for wd in widths],
        compiler_params=_params(2),
        name="to_bf16",
    )(w)
    return out if len(out) > 1 else out[0]


def _ffn_body(h_ref, g_ref, wg_ref, wu_ref, wd_ref, *rest, final, tiles_per_batch):
    if final:
        fg_ref, o_ref, act_ref = rest
    elif tiles_per_batch:
        head_ref, o_ref, act_ref = rest
    else:
        o_ref, act_ref = rest
    x = h_ref[...]
    if tiles_per_batch and not final:
        n_head = head_ref.shape[0]
        first = jnp.concatenate([head_ref[...], x[:x.shape[0] - n_head]], axis=0)
        x = jnp.where(pl.program_id(0) % tiles_per_batch == 0, first, x)
    xn = _rmsnorm(x, g_ref[...]).astype(BF16)
    for c0 in range(0, D_FF, FF_CHUNK):
        c1 = min(c0 + FF_CHUNK, D_FF)
        gate = jnp.dot(xn, wg_ref[:, c0:c1], preferred_element_type=F32)
        up = jnp.dot(xn, wu_ref[:, c0:c1], preferred_element_type=F32)
        act_ref[:, c0:c1] = (gate * _sigmoid(gate) * up).astype(BF16)
    y = x + 0.5 * jnp.dot(act_ref[...], wd_ref[...], preferred_element_type=F32)
    if final:
        y = _rmsnorm(y, fg_ref[...])
    o_ref[...] = y


def _ffn(h, g, wg, wu, wd, l, final_g=None, *, tm, frames=None, head=None):
    rows = h.shape[0]
    final = final_g is not None
    row_spec = pl.BlockSpec((tm, D_MODEL), lambda i: (i, 0))
    in_spec = row_spec
    tiles_per_batch = None
    if frames is not None:
        lp, seq = frames
        if final:
            tiles_per_batch = seq // tm
            rows = rows // lp * seq
            start = lambda i: i // tiles_per_batch * lp + (lp - seq) + i % tiles_per_batch * tm
        else:
            tiles_per_batch = lp // tm
            rows = rows // seq * lp
            start = lambda i: (i // tiles_per_batch * seq
                               + jnp.maximum(i % tiles_per_batch * tm - (lp - seq), 0))
        in_spec = pl.BlockSpec((pl.Element(tm), pl.Element(D_MODEL)),
                               lambda i: (pl.multiple_of(start(i), F32_SUBLANES), 0))
    in_specs = [in_spec, _resident((1, D_MODEL)), _layer(wg, l), _layer(wu, l), _layer(wd, l)]
    args = [h, g, wg, wu, wd]
    if final:
        in_specs.append(_resident((1, D_MODEL)))
        args.append(final_g)
    elif head is not None:
        in_specs.append(_resident(head.shape))
        args.append(head)
    return pl.pallas_call(
        functools.partial(_ffn_body, final=final, tiles_per_batch=tiles_per_batch),
        grid=(rows // tm,),
        in_specs=in_specs,
        out_specs=row_spec,
        out_shape=jax.ShapeDtypeStruct((rows, D_MODEL), F32),
        scratch_shapes=[pltpu.VMEM((tm, D_FF), BF16)],
        compiler_params=_params(1),
        name="ffn_final" if final else "ffn",
    )(*args)


def _proj_body(h_ref, g_ref, w_ref, q_ref, k_ref, v_ref, u_ref, *, tm, pad):
    xn = _rmsnorm(h_ref[...], g_ref[...]).astype(BF16)

    def col(i):
        return jnp.dot(xn, w_ref[:, i * SB_WIDTH:(i + 1) * SB_WIDTH], preferred_element_type=F32)

    q_ref[...] = (col(0) * (-1.0 / math.sqrt(SB_HEAD_DIM))).astype(BF16)
    k_ref[...] = col(1).astype(BF16)
    v_ref[...] = col(2).astype(BF16)
    u = col(3) * _sigmoid(col(4))
    pos = pl.program_id(1) * tm + lax.broadcasted_iota(jnp.int32, (tm, 1), 0)
    u_ref[...] = jnp.where(pos >= pad, u, 0.0)


def _proj(h, g, w, l, *, batch, lp, pad, tm):
    rows = h.shape[0]
    nt = lp // tm
    row = lambda width: pl.BlockSpec((tm, width), lambda b, j: (b * nt + j, 0))
    bf = jax.ShapeDtypeStruct((rows, SB_WIDTH), BF16)
    return pl.pallas_call(
        functools.partial(_proj_body, tm=tm, pad=pad),
        grid=(batch, nt),
        in_specs=[row(D_MODEL), _resident((1, D_MODEL)), _layer(w, l)],
        out_specs=[row(SB_WIDTH), row(SB_WIDTH), row(SB_WIDTH), row(CONV_CH)],
        out_shape=[bf, bf, bf, jax.ShapeDtypeStruct((rows, CONV_CH), F32)],
        compiler_params=_params(2),
        name="mixer_proj",
    )(h, g, w)


def _attn_body(q_ref, k_ref, v_ref, o_ref, acc_ref, carry_ref, *, pad):
    blk = ATT_BLOCK
    i0 = pl.program_id(1) * ATT_QBLOCKS
    row = lax.broadcasted_iota(jnp.int32, (blk, blk), 0)
    lane = lax.broadcasted_iota(jnp.int32, (blk, blk), 1)
    half = jnp.concatenate([(row > lane).astype(BF16), jnp.ones((blk, blk), BF16)], axis=1)
    suffix_and_total = jnp.concatenate([half, half], axis=0)
    first = (lane < SB_HEAD_DIM).astype(BF16)
    second = (lane >= SB_HEAD_DIM).astype(BF16)
    nt = (((1,), (1,)), ((), ()))

    pair_cols = [slice(m * HEAD_PAIR, (m + 1) * HEAD_PAIR) for m in range(SB_HEADS // 2)]
    head_groups = [range(g, g + SB_HEADS // ATT_HEAD_GROUPS)
                   for g in range(0, SB_HEADS, SB_HEADS // ATT_HEAD_GROUPS)]
    block_rows = lambda t: slice(t * blk, (t + 1) * blk)

    def add_rows(old, new):
        if old is None:
            assert new.shape[0] == blk
            return new
        nr = new.shape[0]
        top = old[:nr] + new
        return top if nr == blk else jnp.concatenate([top, old[nr:]], axis=0)

    def sweep(units, carry, acc):
        keys, masked = {}, []
        for t, key, j, nr in units:
            if key not in keys:
                start = pl.multiple_of(jnp.maximum(j, 0) * blk, blk)
                keys[key] = (start, j, k_ref[pl.ds(start, blk), :], v_ref[pl.ds(start, blk), :])
            start, jj = keys[key][:2]
            valid = (start + lane < (i0 + t) * blk + row) & (start + lane >= pad) & (jj >= 0)
            masked.append(jnp.where(valid, 0.0, MASKED)[:nr])
        visitors = {key: [u for u, unit in enumerate(units) if unit[1] == key] for key in keys}
        q_pair = {t: [(q_ref[block_rows(t), c] * first, q_ref[block_rows(t), c] * second)
                      for c in pair_cols] for t in {unit[0] for unit in units}}

        def per_visitor(key, m, stacked):
            out, r0 = {}, 0
            for u in visitors[key]:
                nr = units[u][3]
                out[u, m] = stacked[r0:r0 + 2 * nr]
                r0 += 2 * nr
            return out

        zneg = {}
        for key, (_, _, kt, _) in keys.items():
            for m, c in enumerate(pair_cols):
                lhs = [q[:units[u][3]] for u in visitors[key] for q in q_pair[units[u][0]][m]]
                zneg.update(per_visitor(key, m, lax.dot_general(
                    jnp.concatenate(lhs, axis=0), kt[:, c], nt, preferred_element_type=F32)))
        log_beta, sums = {}, {}
        for heads in head_groups:
            split = []
            for u, (_, _, _, nr) in enumerate(units):
                for h in heads:
                    zn = zneg[u, h // 2][(h % 2) * nr:(h % 2 + 1) * nr] + masked[u]
                    lk = jnp.minimum(zn, 0.0) - jnp.log(1.0 + jnp.exp2(jnp.abs(zn) * (-LOG2E)))
                    hi = lk.astype(BF16)
                    lo = (lk - hi.astype(F32)).astype(BF16)
                    log_beta[u, h] = lk - zn
                    split.append(jnp.concatenate([hi, lo], axis=1))
            group_sums = jnp.dot(jnp.concatenate(split, axis=0), suffix_and_total,
                                 preferred_element_type=F32)
            r0 = 0
            for u, (_, _, _, nr) in enumerate(units):
                for h in heads:
                    sums[u, h] = group_sums[r0:r0 + nr]
                    r0 += nr

        carry = {t: list(c) for t, c in carry.items()}
        acc = {t: list(a) for t, a in acc.items()}
        probs, carry_after = {}, []
        for u, (t, _, _, nr) in enumerate(units):
            for h in range(SB_HEADS):
                after = sums[u, h][:, :blk]
                if carry[t][h] is not None:
                    after = after + carry[t][h][:nr]
                probs[u, h] = jnp.exp(log_beta[u, h] + after).astype(BF16)
                carry[t][h] = add_rows(carry[t][h], sums[u, h][:, blk:])
            carry_after.append(list(carry[t]))
        outs = {}
        for key, (_, _, _, vt) in keys.items():
            for m, c in enumerate(pair_cols):
                lhs = [probs[u, 2 * m + s] for u in visitors[key] for s in range(2)]
                outs.update(per_visitor(key, m, jnp.dot(jnp.concatenate(lhs, axis=0), vt[:, c],
                                                        preferred_element_type=F32)))
        states = []
        for u, (t, _, _, nr) in enumerate(units):
            own = lax.broadcasted_iota(jnp.int32, (nr, HEAD_PAIR), 1) < SB_HEAD_DIM
            for m in range(SB_HEADS // 2):
                acc[t][m] = add_rows(acc[t][m], jnp.where(own, outs[u, m][:nr], outs[u, m][nr:]))
            states.append((carry_after[u], list(acc[t])))
        return states

    def store_state(t, carry, acc):
        for h in range(SB_HEADS):
            carry_ref[t, h] = carry[h]
        for m, c in enumerate(pair_cols):
            acc_ref[block_rows(t), c] = acc[m]

    def is_live(carry):
        return (jnp.max(functools.reduce(jnp.maximum, carry)) > SKIP_LOG).astype(jnp.int32)

    query_blocks = range(ATT_QBLOCKS)
    fixed = [(t, t - d, i0 + t - d, blk if d < ATT_FIXED - 1 else ATT_TAIL_ROWS)
             for t in query_blocks for d in range(ATT_FIXED)]
    states = sweep(fixed, {t: [None] * SB_HEADS for t in query_blocks},
                   {t: [None] * (SB_HEADS // 2) for t in query_blocks})
    live = []
    for t in query_blocks:
        store_state(t, *states[ATT_FIXED * t + ATT_FIXED - 2])
        carry, acc = states[ATT_FIXED * t + ATT_FIXED - 1]
        live.append(is_live(carry))
        o_ref[block_rows(t), :] = jnp.concatenate(acc, axis=1).astype(BF16)

    for t in query_blocks:
        def body(state, t=t):
            j, _ = state
            carry, acc = sweep([(t, 0, j, blk)], {t: [carry_ref[t, h] for h in range(SB_HEADS)]},
                               {t: [acc_ref[block_rows(t), c] for c in pair_cols]})[-1]
            store_state(t, carry, acc)
            return j - 1, is_live(carry)

        lax.while_loop(lambda s: (s[0] >= 0) & (s[1] > 0), body,
                       (i0 + t - (ATT_FIXED - 1), live[t]))

        @pl.when(live[t] > 0)
        def _(t=t):
            o_ref[block_rows(t), :] = acc_ref[block_rows(t), :].astype(BF16)


def _attention(q, k, v, *, batch, lp, pad):
    steps = lp // (ATT_BLOCK * ATT_QBLOCKS)
    assert lp == steps * ATT_BLOCK * ATT_QBLOCKS
    qspec = pl.BlockSpec((ATT_QBLOCKS * ATT_BLOCK, SB_WIDTH), lambda b, g: (b * steps + g, 0))
    kvspec = pl.BlockSpec((lp, SB_WIDTH), lambda b, g: (b, 0))
    return pl.pallas_call(
        functools.partial(_attn_body, pad=pad),
        grid=(batch, steps),
        in_specs=[qspec, kvspec, kvspec],
        out_specs=qspec,
        out_shape=jax.ShapeDtypeStruct(q.shape, BF16),
        scratch_shapes=[pltpu.VMEM((ATT_QBLOCKS * ATT_BLOCK, SB_WIDTH), F32),
                        pltpu.VMEM((ATT_QBLOCKS, SB_HEADS, ATT_BLOCK, ATT_BLOCK), F32)],
        compiler_params=_params(2),
        name="sb_attention",
    )(q, k, v)


def _mix_body(h_ref, g_ref, wgate_ref, a_ref, u_ref, uprev_ref, cw_ref, cb_ref, lng_ref, lnb_ref,
              wao_ref, wco_ref, wo_ref, o_ref, ubuf_ref, shift_ref, conv_ref, *, tm):
    ubuf_ref[0:CONV_HALO, :] = uprev_ref[...]
    ubuf_ref[CONV_HALO:, :] = u_ref[...]
    base = CONV_HALO - (CONV_K - 1)
    sub = F32_SUBLANES
    for r in range(sub):
        n = tm + sub * ((CONV_K - 1 - r) // sub)
        shift_ref[r, 0:n // sub] = ubuf_ref[base + r:base + r + n, :].reshape(n // sub, sub, CONV_CH)

    def conv_chunk(c0):
        y = jnp.broadcast_to(cb_ref[...], (CONV_CHUNK // sub, sub, CONV_CH))
        for k in range(CONV_K):
            g0 = (c0 + k - k % sub) // sub
            y = y + cw_ref[k] * shift_ref[k % sub, g0:g0 + CONV_CHUNK // sub]
        y = y.reshape(CONV_CHUNK, CONV_CH)
        mu = jnp.mean(y, axis=-1, keepdims=True)
        yc = y - mu
        var = jnp.mean(yc * yc, axis=-1, keepdims=True)
        ln = yc * lax.rsqrt(var + LN_EPS) * lng_ref[...] + lnb_ref[...]
        conv_ref[c0:c0 + CONV_CHUNK, :] = (ln * _sigmoid(ln)).astype(BF16)

    for r0 in range(0, tm, MIX_ROWS):
        rows = slice(r0, r0 + MIX_ROWS)
        for c0 in range(r0, r0 + MIX_ROWS, CONV_CHUNK):
            conv_chunk(c0)
        x = h_ref[rows, :]
        xn = _rmsnorm(x, g_ref[...]).astype(BF16)
        gate_sb = _sigmoid(jnp.dot(xn, wgate_ref[:, :D_MODEL], preferred_element_type=F32))
        gate_cv = _sigmoid(jnp.dot(xn, wgate_ref[:, D_MODEL:], preferred_element_type=F32))
        y_sb = jnp.dot(a_ref[rows, :], wao_ref[...], preferred_element_type=F32)
        y_cv = jnp.dot(conv_ref[rows, :], wco_ref[...], preferred_element_type=F32)
        m = (gate_sb * y_sb + gate_cv * y_cv).astype(BF16)
        o_ref[rows, :] = x + jnp.dot(m, wo_ref[...], preferred_element_type=F32)


def _mix(h, g, wgate, a, u, cw, cb, lng, lnb, wao, wco, wo, l, *, tm):
    rows = h.shape[0]
    halo_per_tile = tm // CONV_HALO
    row = lambda width: pl.BlockSpec((tm, width), lambda i: (i, 0))
    prev = pl.BlockSpec((CONV_HALO, CONV_CH), lambda i: (jnp.maximum(i * halo_per_tile - 1, 0), 0))
    shift_rows = tm + F32_SUBLANES * ((CONV_K - 1) // F32_SUBLANES)
    return pl.pallas_call(
        functools.partial(_mix_body, tm=tm),
        grid=(rows // tm,),
        in_specs=[row(D_MODEL), _resident((1, D_MODEL)), _layer(wgate, l), row(SB_WIDTH),
                  row(CONV_CH), prev, _resident(cw.shape), _resident((1, CONV_CH)),
                  _resident((1, CONV_CH)), _resident((1, CONV_CH)), _layer(wao, l),
                  _layer(wco, l), _layer(wo, l)],
        out_specs=row(D_MODEL),
        out_shape=jax.ShapeDtypeStruct((rows, D_MODEL), F32),
        scratch_shapes=[pltpu.VMEM((CONV_HALO + tm, CONV_CH), F32),
                        pltpu.VMEM((F32_SUBLANES, shift_rows // F32_SUBLANES, F32_SUBLANES, CONV_CH), F32),
                        pltpu.VMEM((tm, CONV_CH), BF16)],
        compiler_params=_params(1),
        name="mixer_out",
    )(h, g, wgate, a, u, u, cw, cb, lng, lnb, wao, wco, wo)


def kernel(x, meta, ffn1_norm, ffn1_w_gate, ffn1_w_up, ffn1_w_down, mix_norm, w_in, w_attn_out,
           conv_w, conv_b, conv_ln_g, conv_ln_b, w_conv_out, w_out, ffn2_norm, ffn2_w_gate,
           ffn2_w_up, ffn2_w_down, final_norm):
    batch, seq, d = x.shape
    depth = w_in.shape[0]
    assert d == D_MODEL and meta.shape == (N_META, D_MODEL) and seq % ATT_BLOCK == 0
    assert N_META <= ATT_BLOCK
    pad = ATT_BLOCK - N_META
    lp = seq + ATT_BLOCK
    tm_ffn = 1040
    tm_mix = 640
    tm_out = 1024
    assert lp % tm_ffn == 0 and lp % tm_mix == 0 and tm_mix % CONV_HALO == 0
    assert tm_mix % CONV_CHUNK == 0 and seq % tm_out == 0

    head = jnp.concatenate([jnp.zeros((pad, d), x.dtype), meta.astype(x.dtype)], axis=0)
    h = x.reshape(batch * seq, d)

    vec = lambda p: p.reshape(1, -1).astype(F32)
    conv_taps = lambda w: jnp.broadcast_to(w.astype(F32)[:, None, :], (CONV_K, F32_SUBLANES, CONV_CH))
    ffn1 = [_to_bf16(w) for w in (ffn1_w_gate, ffn1_w_up, ffn1_w_down)]
    ffn2 = [_to_bf16(w) for w in (ffn2_w_gate, ffn2_w_up, ffn2_w_down)]
    w_proj, w_gates = _to_bf16(w_in, (3 * SB_WIDTH + 2 * CONV_CH, w_in.shape[2]))
    w_ao, w_co, w_o = _to_bf16(w_attn_out), _to_bf16(w_conv_out), _to_bf16(w_out)
    for l in range(depth):
        ends = dict(frames=(lp, seq), head=head) if l == 0 else {}
        h = _ffn(h, vec(ffn1_norm[l]), *ffn1, l, tm=tm_ffn, **ends)
        q, k, v, u = _proj(h, vec(mix_norm[l]), w_proj, l, batch=batch, lp=lp, pad=pad, tm=tm_ffn)
        a = _attention(q, k, v, batch=batch, lp=lp, pad=pad)
        h = _mix(h, vec(mix_norm[l]), w_gates, a, u, conv_taps(conv_w[l]), vec(conv_b[l]),
                 vec(conv_ln_g[l]), vec(conv_ln_b[l]), w_ao, w_co, w_o, l, tm=tm_mix)
        if l < depth - 1:
            h = _ffn(h, vec(ffn2_norm[l]), *ffn2, l, tm=tm_ffn)
    out = _ffn(h, vec(ffn2_norm[depth - 1]), *ffn2, depth - 1, vec(final_norm), tm=tm_out,
               frames=(lp, seq))
    return out.reshape(batch, seq, d)
```
